```python
import math, functools
import jax, jax.numpy as jnp
from jax import lax
import numpy as np

D_MODEL = 1024
BATCH = 1
SEQ = 16384
DEPTH = 2
DEC_BATCH = 32
DEC_SEQ = 8
PAST_LEN = 16384
PAGE_SIZE = 128

D_RNN = D_MODEL
N_RNN_BLOCKS = 16
RNN_BLOCK_W = D_RNN // N_RNN_BLOCKS
CONV_W = 4
LRU_C = 8.0
N_HEADS = 8
N_KV_HEADS = 4
HEAD_DIM = D_MODEL // N_HEADS
D_Q = N_HEADS * HEAD_DIM
D_KV = N_KV_HEADS * HEAD_DIM
MOBA_BLOCK = 256
MOBA_TOP_K = 3
Q_CHUNK = 64
N_BUCKETS = 32
MAX_DISTANCE = 128
D_FF = 2816
N_ADA = 9
EPS = 1e-6
D_IN = 2 * D_RNN + D_Q + 2 * D_KV + 2 * D_MODEL

kernel_name = 'hybrid_rglru_moba_decoder_step'


def rmsnorm(x, g):
    x32 = x.astype(jnp.float32)
    y = x32 * lax.rsqrt(jnp.mean(x32 * x32, axis=-1, keepdims=True) + EPS)
    return (y * g.astype(jnp.float32)).astype(x.dtype)


def swiglu(h, w_up, w_down):
    a, g = jnp.split(h @ w_up, 2, axis=-1)
    return (jax.nn.silu(g) * a) @ w_down


def rel_bucket(dist):
    n = jnp.maximum(dist, 0)
    max_exact = N_BUCKETS // 2
    nf = jnp.maximum(n, 1).astype(jnp.float32)
    large = max_exact + (jnp.log(nf / max_exact) / math.log(MAX_DISTANCE / max_exact)
                         * (N_BUCKETS - max_exact)).astype(jnp.int32)
    large = jnp.minimum(large, N_BUCKETS - 1)
    return jnp.where(n < max_exact, n, large)


def kv_head_of_query_head():
    return jnp.arange(N_HEADS) // (N_HEADS // N_KV_HEADS)


def lru_combine(left, right):
    a1, b1 = left
    a2, b2 = right
    return a1 * a2, a2 * b1 + b2


def rglru_branch(xr, gr, conv_buf, h0, conv_w, conv_b, w_a, b_a, w_x, b_x, lru_lambda):
    B, L, _ = xr.shape
    f32 = jnp.float32
    xcat = jnp.concatenate([conv_buf.astype(xr.dtype), xr], axis=1)
    xc = conv_b + sum(conv_w[j] * xcat[:, j:j + L] for j in range(CONV_W))
    new_buf = xcat[:, L:]
    xb = xc.reshape(B, L, N_RNN_BLOCKS, RNN_BLOCK_W)
    r = jax.nn.sigmoid((jnp.einsum('blnc,ncd->blnd', xb, w_a).reshape(B, L, D_RNN) + b_a).astype(f32))
    i = jax.nn.sigmoid((jnp.einsum('blnc,ncd->blnd', xb, w_x).reshape(B, L, D_RNN) + b_x).astype(f32))
    log_a = -LRU_C * r * jax.nn.softplus(-lru_lambda.astype(f32))
    a = jnp.exp(log_a)
    mult = jnp.sqrt(-jnp.expm1(2.0 * log_a))
    b = mult * i * xc.astype(f32)
    b = b.at[:, 0].add(a[:, 0] * h0.astype(f32))
    _, h = lax.associative_scan(lru_combine, (a, b), axis=1)
    y = h.astype(xr.dtype) * jax.nn.gelu(gr)
    return y, new_buf, h[:, -1]


def moba_core(q, qpos, kmean, gather_blocks, own_k, own_v, own_pos, rel_table):
    f32 = jnp.float32
    kvh = kv_head_of_query_head()
    cq = q.shape[0]
    own_vh = own_v[:, kvh]
    lo = jnp.einsum('qhd,khd->qhk', q, own_k[:, kvh], preferred_element_type=f32)
    bo = jnp.transpose(rel_table[rel_bucket(qpos[:, None] - own_pos[None, :])], (0, 2, 1))
    lo = jnp.where(own_pos[None, None, :] <= qpos[:, None, None], lo + bo, -jnp.inf)
    nb = kmean.shape[0]
    if nb == 0:
        p = jax.nn.softmax(lo, axis=-1)
        return jnp.einsum('qhk,khd->qhd', p, own_vh, preferred_element_type=f32)
    gs = jnp.einsum('qhd,nhd->qhn', q, kmean[:, kvh], preferred_element_type=f32)
    blk = qpos // MOBA_BLOCK
    gs = jnp.where(jnp.arange(nb)[None, None, :] < blk[:, None, None], gs, -jnp.inf)
    n_sel = min(MOBA_TOP_K, nb)
    _, idx = lax.top_k(gs, n_sel)
    valid = jnp.arange(n_sel)[None, None, :] < blk[:, None, None]
    ks, vs = gather_blocks(idx)
    ls = jnp.einsum('qhd,qhnkd->qhnk', q, ks, preferred_element_type=f32)
    kpos = idx[..., None] * MOBA_BLOCK + jnp.arange(MOBA_BLOCK)
    bs = rel_table[rel_bucket(qpos[:, None, None, None] - kpos), jnp.arange(N_HEADS)[None, :, None, None]]
    ls = jnp.where(valid[..., None], ls + bs, -jnp.inf).reshape(cq, N_HEADS, n_sel * MOBA_BLOCK)
    p = jax.nn.softmax(jnp.concatenate([ls, lo], axis=-1), axis=-1)
    ps = p[..., :n_sel * MOBA_BLOCK].reshape(cq, N_HEADS, n_sel, MOBA_BLOCK)
    po = p[..., n_sel * MOBA_BLOCK:]
    return (jnp.einsum('qhnk,qhnkd->qhd', ps, vs, preferred_element_type=f32)
            + jnp.einsum('qhk,khd->qhd', po, own_vh, preferred_element_type=f32))


def moba_prompt_seq(q, k, v, rel_table):
    S = q.shape[0]
    nb = S // MOBA_BLOCK
    kvh = kv_head_of_query_head()
    k_blocks = k[:nb * MOBA_BLOCK].reshape(nb, MOBA_BLOCK, N_KV_HEADS, HEAD_DIM)
    v_blocks = v[:nb * MOBA_BLOCK].reshape(nb, MOBA_BLOCK, N_KV_HEADS, HEAD_DIM)
    kmean = jnp.mean(k_blocks.astype(jnp.float32), axis=1)
    pad = jnp.zeros((MOBA_BLOCK, N_KV_HEADS, HEAD_DIM), k.dtype)
    k_pad = jnp.concatenate([k, pad], axis=0)
    v_pad = jnp.concatenate([v, pad.astype(v.dtype)], axis=0)

    def gather_blocks(idx):
        h = kvh[None, :, None]
        return k_blocks[idx, :, h], v_blocks[idx, :, h]

    n_chunks = S // Q_CHUNK

    def one_chunk(args):
        q_c, c = args
        start = c * Q_CHUNK
        qpos = start + jnp.arange(Q_CHUNK)
        own_start = (start // MOBA_BLOCK) * MOBA_BLOCK
        own_k = lax.dynamic_slice_in_dim(k_pad, own_start, MOBA_BLOCK, axis=0)
        own_v = lax.dynamic_slice_in_dim(v_pad, own_start, MOBA_BLOCK, axis=0)
        own_pos = own_start + jnp.arange(MOBA_BLOCK)
        return moba_core(q_c, qpos, kmean, gather_blocks, own_k, own_v, own_pos, rel_table)

    out = lax.map(one_chunk, (q.reshape(n_chunks, Q_CHUNK, N_HEADS, HEAD_DIM), jnp.arange(n_chunks)))
    return out.reshape(S, N_HEADS, HEAD_DIM)


def moba_sample_seq(args, layer, cache_k, cache_v, rel_table):
    q, k_new, v_new, pt = args
    L = q.shape[0]
    ppb = MOBA_BLOCK // PAGE_SIZE
    nb = PAST_LEN // MOBA_BLOCK
    own_past = PAST_LEN - nb * MOBA_BLOCK
    kvh = kv_head_of_query_head()
    past_k_full = cache_k[layer, pt[:nb * ppb]]
    kmean = jnp.mean(past_k_full.astype(jnp.float32).reshape(nb, MOBA_BLOCK, N_KV_HEADS, HEAD_DIM), axis=1)
    own_pages = pt[nb * ppb:]
    own_k = jnp.concatenate([cache_k[layer, own_pages].reshape(own_past, N_KV_HEADS, HEAD_DIM).astype(k_new.dtype), k_new], axis=0)
    own_v = jnp.concatenate([cache_v[layer, own_pages].reshape(own_past, N_KV_HEADS, HEAD_DIM).astype(v_new.dtype), v_new], axis=0)
    own_pos = nb * MOBA_BLOCK + jnp.arange(own_past + L)
    qpos = PAST_LEN + jnp.arange(L)

    def gather_blocks(idx):
        pages = pt[idx[..., None] * ppb + jnp.arange(ppb)]
        h = kvh[None, :, None, None]
        ks = cache_k[layer, pages, :, h].reshape(*idx.shape, MOBA_BLOCK, HEAD_DIM)
        vs = cache_v[layer, pages, :, h].reshape(*idx.shape, MOBA_BLOCK, HEAD_DIM)
        return ks, vs

    return moba_core(q, qpos, kmean, gather_blocks, own_k, own_v, own_pos, rel_table)


def trunk_layer(x, c, p, conv_buf, h0, attend):
    B, L, _ = x.shape
    mod = (jax.nn.silu(c) @ p['w_ada'] + p['b_ada']).reshape(B, N_ADA, 1, D_MODEL)

    def modulate(h, i):
        return h * (1.0 + mod[:, 3 * i + 1]) + mod[:, 3 * i]

    h = modulate(rmsnorm(x, p['norm_g'][0]), 0)
    x = x + 0.5 * mod[:, 2] * swiglu(h, p['w_ffn_in'][0], p['w_ffn_out'][0])

    h = modulate(rmsnorm(x, p['norm_g'][1]), 1)
    u = h @ p['w_in']
    cuts = [D_RNN, 2 * D_RNN, 2 * D_RNN + D_Q, 2 * D_RNN + D_Q + D_KV,
            2 * D_RNN + D_Q + 2 * D_KV, 2 * D_RNN + D_Q + 2 * D_KV + D_MODEL]
    xr, gr, q, k, v, g_a, g_b = jnp.split(u, cuts, axis=-1)
    y_rnn, conv_new, h_new = rglru_branch(xr, gr, conv_buf, h0, p['conv_w'], p['conv_b'],
                                          p['w_rg_a'], p['b_rg_a'], p['w_rg_x'], p['b_rg_x'], p['lru_lambda'])
    q = rmsnorm(q.reshape(B, L, N_HEADS, HEAD_DIM), p['q_norm_g']) * (HEAD_DIM ** -0.5)
    k = rmsnorm(k.reshape(B, L, N_KV_HEADS, HEAD_DIM), p['k_norm_g'])
    v = v.reshape(B, L, N_KV_HEADS, HEAD_DIM)
    y_att = attend(q, k, v).reshape(B, L, D_Q).astype(x.dtype)
    merged = jax.nn.sigmoid(g_a) * y_rnn + jax.nn.sigmoid(g_b) * y_att
    x = x + mod[:, 5] * (merged @ p['w_out'])

    h = modulate(rmsnorm(x, p['norm_g'][2]), 2)
    x = x + 0.5 * mod[:, 8] * swiglu(h, p['w_ffn_in'][1], p['w_ffn_out'][1])
    return x, k, v, conv_new, h_new


def setup_inputs(seed: int = 0) -> dict:
    key = jax.random.key(seed)
    ks = jax.random.split(key, 26)
    f32 = jnp.float32
    n_pages = PAST_LEN // PAGE_SIZE
    n_used = DEC_BATCH * n_pages
    n_pool = n_used + max(1, n_used // 4)

    def nrm(k, shape, s):
        return s * jax.random.normal(k, shape, f32)

    lam_a = jax.random.uniform(ks[19], (DEPTH, D_RNN), f32, minval=0.9, maxval=0.999)
    page_table = jax.random.permutation(ks[6], n_pool)[:n_used].reshape(DEC_BATCH, n_pages).astype(jnp.int32)
    return {
        'x_prompt': nrm(ks[0], (BATCH, SEQ, D_MODEL), 1.0),
        'x_sample': nrm(ks[1], (DEC_BATCH, DEC_SEQ, D_MODEL), 1.0),
        'c_prompt': nrm(ks[2], (BATCH, D_MODEL), 1.0),
        'c_sample': nrm(ks[3], (DEC_BATCH, D_MODEL), 1.0),
        'cache_k': nrm(ks[4], (DEPTH, n_pool, PAGE_SIZE, N_KV_HEADS, HEAD_DIM), 1.0),
        'cache_v': nrm(ks[5], (DEPTH, n_pool, PAGE_SIZE, N_KV_HEADS, HEAD_DIM), 1.0),
        'page_table': page_table,
        'state_conv': nrm(ks[7], (DEPTH, DEC_BATCH, CONV_W - 1, D_RNN), 1.0),
        'state_rglru': nrm(ks[8], (DEPTH, DEC_BATCH, D_RNN), 0.5),
        'norm_g': 1.0 + nrm(ks[9], (DEPTH, 3, D_MODEL), 0.05),
        'w_ada': nrm(ks[10], (DEPTH, D_MODEL, N_ADA * D_MODEL), 0.5 * D_MODEL ** -0.5),
        'b_ada': nrm(ks[11], (DEPTH, N_ADA * D_MODEL), 0.01),
        'w_in': nrm(ks[12], (DEPTH, D_MODEL, D_IN), D_MODEL ** -0.5),
        'conv_w': nrm(ks[13], (DEPTH, CONV_W, D_RNN), CONV_W ** -0.5),
        'conv_b': nrm(ks[14], (DEPTH, D_RNN), 0.01),
        'w_rg_a': nrm(ks[15], (DEPTH, N_RNN_BLOCKS, RNN_BLOCK_W, RNN_BLOCK_W), RNN_BLOCK_W ** -0.5),
        'b_rg_a': nrm(ks[16], (DEPTH, D_RNN), 0.01),
        'w_rg_x': nrm(ks[17], (DEPTH, N_RNN_BLOCKS, RNN_BLOCK_W, RNN_BLOCK_W), RNN_BLOCK_W ** -0.5),
        'b_rg_x': nrm(ks[18], (DEPTH, D_RNN), 0.01),
        'lru_lambda': jnp.log(lam_a) - jnp.log1p(-lam_a),
        'q_norm_g': 1.0 + nrm(ks[20], (DEPTH, HEAD_DIM), 0.05),
        'k_norm_g': 1.0 + nrm(ks[21], (DEPTH, HEAD_DIM), 0.05),
        'rel_table': nrm(ks[22], (N_BUCKETS, N_HEADS), 0.5),
        'w_out': nrm(ks[23], (DEPTH, D_MODEL, D_MODEL), D_MODEL ** -0.5),
        'w_ffn_in': nrm(ks[24], (DEPTH, 2, D_MODEL, 2 * D_FF), D_MODEL ** -0.5),
        'w_ffn_out': nrm(ks[25], (DEPTH, 2, D_FF, D_MODEL), D_FF ** -0.5),
    }


def reference(x_prompt, x_sample, c_prompt, c_sample, cache_k, cache_v, page_table, state_conv,
              state_rglru, norm_g, w_ada, b_ada, w_in, conv_w, conv_b, w_rg_a, b_rg_a, w_rg_x, b_rg_x,
              lru_lambda, q_norm_g, k_norm_g, rel_table, w_out, w_ffn_in, w_ffn_out):
    bp = x_prompt.shape[0]
    conv0 = jnp.zeros((bp, CONV_W - 1, D_RNN), x_prompt.dtype)
    h0 = jnp.zeros((bp, D_RNN), jnp.float32)
    attend_prompt = jax.vmap(lambda q, k, v: moba_prompt_seq(q, k, v, rel_table))
    xp, xs = x_prompt, x_sample
    kp_l, vp_l, cp_l, hp_l, ks_l, vs_l, cs_l, hs_l = [], [], [], [], [], [], [], []
    for l in range(DEPTH):
        p = {
            'norm_g': norm_g[l], 'w_ada': w_ada[l], 'b_ada': b_ada[l], 'w_in': w_in[l],
            'conv_w': conv_w[l], 'conv_b': conv_b[l], 'w_rg_a': w_rg_a[l], 'b_rg_a': b_rg_a[l],
            'w_rg_x': w_rg_x[l], 'b_rg_x': b_rg_x[l], 'lru_lambda': lru_lambda[l],
            'q_norm_g': q_norm_g[l], 'k_norm_g': k_norm_g[l], 'w_out': w_out[l],
            'w_ffn_in': w_ffn_in[l], 'w_ffn_out': w_ffn_out[l],
        }
        sample_seq = functools.partial(moba_sample_seq, layer=l, cache_k=cache_k, cache_v=cache_v,
                                       rel_table=rel_table)

        def attend_sample(q, k, v, sample_seq=sample_seq):
            return lax.map(sample_seq, (q, k, v, page_table))

        xp, kp, vp, cp, hp = trunk_layer(xp, c_prompt, p, conv0, h0, attend_prompt)
        xs, ks_, vs_, cs, hs = trunk_layer(xs, c_sample, p, state_conv[l], state_rglru[l], attend_sample)
        kp_l.append(kp); vp_l.append(vp); cp_l.append(cp); hp_l.append(hp)
        ks_l.append(ks_); vs_l.append(vs_); cs_l.append(cs); hs_l.append(hs)
    k_prompt = jnp.stack(kp_l)
    v_prompt = jnp.stack(vp_l)
    conv_prompt = jnp.stack(cp_l)
    rglru_prompt = jnp.stack(hp_l)
    k_sample = jnp.stack(ks_l)
    v_sample = jnp.stack(vs_l)
    conv_sample = jnp.stack(cs_l)
    rglru_sample = jnp.stack(hs_l)
    return (xp, xs, k_prompt, v_prompt, conv_prompt, rglru_prompt, k_sample, v_sample, conv_sample, rglru_sample)
```

```python
import functools
import math

import jax
import jax.numpy as jnp
from jax import lax
from jax.experimental import pallas as pl
from jax.experimental.pallas import tpu as pltpu

F32 = jnp.float32
BF16 = jnp.bfloat16

EPS = 1e-6
LRU_C = 8.0
MOBA_BLOCK = 256
MOBA_TOP_K = 3
MAX_DISTANCE = 128
NEG = -1e30

V7X_VMEM_BYTES = 64 * 1024 * 1024
VMEM_LIMIT = V7X_VMEM_BYTES * 3 // 4
SUBLANES = 8
PAGES_PER_STEP = 16


def _cparams(sem):
    return pltpu.CompilerParams(dimension_semantics=sem, vmem_limit_bytes=VMEM_LIMIT)


def _resident(shape):
    nd = len(shape)
    return pl.BlockSpec(shape, lambda *_: (0,) * nd, pipeline_mode=pl.Buffered(1))


def _sigmoid(x):
    return 1.0 / (1.0 + jnp.exp(-x))


def _gelu_tanh(x):
    return 0.5 * x * (1.0 + jnp.tanh(math.sqrt(2.0 / math.pi) * (x + 0.044715 * (x * x * x))))


def _rms_mod(x, g, shift, scale):
    ms = jnp.mean(x * x, axis=-1, keepdims=True)
    y = x * lax.rsqrt(ms + EPS) * g
    return y * (1.0 + scale) + shift


def _dot(a, b):
    return jnp.dot(a, b, preferred_element_type=F32)


def _dot_nt(a, b):
    return lax.dot_general(a, b, (((1,), (1,)), ((), ())), preferred_element_type=F32)


def _dot_f32(a, b):
    return jnp.dot(a, b, preferred_element_type=F32, precision=lax.Precision.HIGHEST)


def _dot_nt_f32(a, b):
    return lax.dot_general(a, b, (((1,), (1,)), ((), ())), preferred_element_type=F32,
                           precision=lax.Precision.HIGHEST)


def _ada_kernel(c_ref, w_ref, b_ref, o_ref):
    c = c_ref[...]
    o_ref[...] = _dot_f32(c * _sigmoid(c), w_ref[...]) + b_ref[...]


def _ada_call(c_all, w_ada, b_ada):
    depth, d, n = w_ada.shape
    rows = c_all.shape[0]
    tn = 1024
    return pl.pallas_call(
        _ada_kernel,
        out_shape=jax.ShapeDtypeStruct((depth, rows, n), F32),
        grid=(depth, n // tn),
        in_specs=[
            pl.BlockSpec((rows, d), lambda l, j: (0, 0)),
            pl.BlockSpec((None, d, tn), lambda l, j: (l, 0, j)),
            pl.BlockSpec((None, 1, tn), lambda l, j: (l, 0, j)),
        ],
        out_specs=pl.BlockSpec((None, rows, tn), lambda l, j: (l, 0, j)),
        compiler_params=_cparams(("arbitrary", "arbitrary")),
        name="adaln",
    )(c_all, w_ada, b_ada.reshape(depth, 1, n))


def _mod_spec(mod, tm, sub, d):
    if mod.shape[0] == 1:
        return pl.BlockSpec((1, 3 * d), lambda i: (0, sub))
    return pl.BlockSpec((tm, 3 * d), lambda i: (i, sub))


def _ffn_body(x, mod_ref, g_ref, wup_ref, wdn_ref, fc):
    d = x.shape[-1]
    f = wdn_ref.shape[0]
    h = _rms_mod(x, g_ref[...], mod_ref[:, 0:d], mod_ref[:, d:2 * d]).astype(BF16)
    acc = jnp.zeros(x.shape, F32)
    for c in range(f // fc):
        a = _dot(h, wup_ref[:, c * fc:(c + 1) * fc])
        g = _dot(h, wup_ref[:, f + c * fc:f + (c + 1) * fc])
        act = (g * _sigmoid(g) * a).astype(BF16)
        acc = acc + _dot(act, wdn_ref[c * fc:(c + 1) * fc, :])
    return x + 0.5 * mod_ref[:, 2 * d:3 * d] * acc


def _ffn_kernel(x_ref, mod_ref, g_ref, wup_ref, wdn_ref, o_ref, *, fc):
    o_ref[...] = _ffn_body(x_ref[...], mod_ref, g_ref, wup_ref, wdn_ref, fc)


def _outproj_ffn_kernel(x_ref, ya_ref, sgb_ref, yatt_ref, wout_ref, modo_ref, mod_ref, g_ref, wup_ref, wdn_ref,
                        o_ref, *, fc):
    x = x_ref[...]
    d = x.shape[-1]
    merged = (ya_ref[...] + sgb_ref[...] * yatt_ref[...]).astype(BF16)
    x = x + modo_ref[:, 2 * d:3 * d] * _dot(merged, wout_ref[...])
    o_ref[...] = _ffn_body(x, mod_ref, g_ref, wup_ref, wdn_ref, fc)


def _ffn_chunk(f):
    for fc in (512, 256, 128):
        if f % fc == 0:
            return fc
    return f


def _ffn_call(x, mod, sub, g, wup, wdn, tm):
    t, d = x.shape
    row = pl.BlockSpec((tm, d), lambda i: (i, 0))
    return pl.pallas_call(
        functools.partial(_ffn_kernel, fc=_ffn_chunk(wdn.shape[0])),
        out_shape=jax.ShapeDtypeStruct((t, d), F32),
        grid=(t // tm,),
        in_specs=[row, _mod_spec(mod, tm, sub, d), _resident((1, d)), _resident(wup.shape), _resident(wdn.shape)],
        out_specs=row,
        compiler_params=_cparams(("parallel",)),
        name="ffn",
    )(x, mod, g.reshape(1, d), wup, wdn)


def _outproj_ffn_call(x, ya, sgb, yatt, wout, mod, sub_out, sub, g, wup, wdn, tm):
    t, d = x.shape
    row = pl.BlockSpec((tm, d), lambda i: (i, 0))
    return pl.pallas_call(
        functools.partial(_outproj_ffn_kernel, fc=_ffn_chunk(wdn.shape[0])),
        out_shape=jax.ShapeDtypeStruct((t, d), F32),
        grid=(t // tm,),
        in_specs=[row, row, row, row, _resident(wout.shape), _mod_spec(mod, tm, sub_out, d),
                  _mod_spec(mod, tm, sub, d), _resident((1, d)), _resident(wup.shape), _resident(wdn.shape)],
        out_specs=row,
        compiler_params=_cparams(("parallel",)),
        name="outproj_ffn",
    )(x, ya, sgb, yatt, wout, mod, mod, g.reshape(1, d), wup, wdn)


def _head_rms(u, g, nheads, hd, scale):
    outs = []
    for h in range(nheads):
        uh = u[:, h * hd:(h + 1) * hd]
        ms = jnp.mean(uh * uh, axis=-1, keepdims=True)
        outs.append(uh * lax.rsqrt(ms + EPS) * g * scale)
    return outs


def _inproj_kernel(x_ref, mod_ref, g_ref, win_ref, qg_ref, kg_ref, *out_refs, dr, nh, nkv, hd, prompt):
    if prompt:
        xr_ref, gg_ref, sgb_ref, q_ref, k_ref, v_ref, kb_ref, vt_ref, km_ref = out_refs
    else:
        xr_ref, gg_ref, sgb_ref, q_ref, k_ref, v_ref = out_refs
    x = x_ref[...]
    tm, d = x.shape
    h = _rms_mod(x, g_ref[...], mod_ref[:, 0:d], mod_ref[:, d:2 * d]).astype(BF16)
    dq, dkv = nh * hd, nkv * hd
    o_q = 2 * dr
    o_k = o_q + dq
    o_v = o_k + dkv
    o_ga = o_v + dkv
    o_gb = o_ga + d

    def seg(lo, width):
        return _dot(h, win_ref[:, lo:lo + width])

    xr_ref[...] = seg(0, dr)
    gg_ref[...] = _sigmoid(seg(o_ga, d)) * _gelu_tanh(seg(dr, dr))
    sgb_ref[...] = _sigmoid(seg(o_gb, d))
    qs = _head_rms(seg(o_q, dq), qg_ref[...], nh, hd, hd ** -0.5)
    for hh in range(nh):
        q_ref[:, hh * hd:(hh + 1) * hd] = qs[hh]
    ks = _head_rms(seg(o_k, dkv), kg_ref[...], nkv, hd, 1.0)
    v = seg(o_v, dkv)
    v_ref[...] = v
    for hh in range(nkv):
        k_ref[:, hh * hd:(hh + 1) * hd] = ks[hh]
    if prompt:
        vt_ref[...] = v.T.astype(BF16)
        for hh in range(nkv):
            kb_ref[:, hh * hd:(hh + 1) * hd] = ks[hh].astype(BF16)
            for b in range(tm // MOBA_BLOCK):
                km_ref[b, :, hh * hd:(hh + 1) * hd] = jnp.mean(
                    ks[hh][b * MOBA_BLOCK:(b + 1) * MOBA_BLOCK], axis=0, keepdims=True)


def _inproj_call(x, mod, g, win, qg, kg, tm, *, dr, nh, nkv, hd, prompt):
    t, d = x.shape
    dq, dkv = nh * hd, nkv * hd
    row = lambda w: pl.BlockSpec((tm, w), lambda i: (i, 0))
    out_shape = [jax.ShapeDtypeStruct((t, w), F32) for w in (dr, d, d, dq, dkv, dkv)]
    out_specs = [row(w) for w in (dr, d, d, dq, dkv, dkv)]
    if prompt:
        nbt = tm // MOBA_BLOCK
        out_shape += [jax.ShapeDtypeStruct((t, dkv), BF16), jax.ShapeDtypeStruct((dkv, t), BF16),
                      jax.ShapeDtypeStruct((t // MOBA_BLOCK, 1, dkv), F32)]
        out_specs += [row(dkv), pl.BlockSpec((dkv, tm), lambda i: (0, i)),
                      pl.BlockSpec((nbt, 1, dkv), lambda i: (i, 0, 0))]
    return pl.pallas_call(
        functools.partial(_inproj_kernel, dr=dr, nh=nh, nkv=nkv, hd=hd, prompt=prompt),
        out_shape=out_shape,
        grid=(t // tm,),
        in_specs=[row(d), _mod_spec(mod, tm, 1, d), _resident((1, d)), _resident(win.shape),
                  _resident((1, hd)), _resident((1, hd))],
        out_specs=out_specs,
        compiler_params=_cparams(("parallel",)),
        name="inproj",
    )(x, mod, g.reshape(1, d), win, qg.reshape(1, hd), kg.reshape(1, hd))


def _rglru_gates(xc, wg_ref, ba_ref, bx_ref, lam_ref, a_scr, b_scr):
    ngroups, gw, _ = wg_ref.shape
    xcb = xc.astype(BF16)
    for gi in range(ngroups):
        cols = slice(gi * gw, (gi + 1) * gw)
        u = _dot(xcb[:, cols], wg_ref[gi])
        r = _sigmoid(u[:, :gw] + ba_ref[:, cols])
        gate_x = _sigmoid(u[:, gw:] + bx_ref[:, cols])
        lam = lam_ref[:, cols]
        softplus_neg = jnp.maximum(-lam, 0.0) + jnp.log(1.0 + jnp.exp(-jnp.abs(lam)))
        log_a = -LRU_C * r * softplus_neg
        a = jnp.exp(log_a)
        mult = jnp.sqrt(-_expm1(2.0 * log_a))
        a_scr[:, cols] = a
        b_scr[:, cols] = mult * gate_x * xc[:, cols]


def _expm1(x):
    small = x * (1.0 + x * (0.5 + x * (1.0 / 6.0 + x * (1.0 / 24.0 + x * (1.0 / 120.0)))))
    return jnp.where(x > -0.1, small, jnp.exp(x) - 1.0)


def _scan8(a, b):
    row = lax.broadcasted_iota(jnp.int32, a.shape, 0)
    for k in (1, 2, 4):
        keep = row >= k
        a_sh = jnp.where(keep, pltpu.roll(a, k, 0), 1.0)
        b_sh = jnp.where(keep, pltpu.roll(b, k, 0), 0.0)
        b = a * b_sh + b
        a = a * a_sh
    return a, b


def _rglru_prompt_kernel(xr_ref, gg_ref, cw_ref, cb_ref, wg_ref, ba_ref, bx_ref, lam_ref, ya_ref, hlast_ref,
                         xbuf, a_scr, b_scr, hcar):
    i = pl.program_id(0)
    tm, dr = xr_ref.shape
    cwid = cw_ref.shape[0]

    @pl.when(i == 0)
    def _():
        xbuf[0:SUBLANES, :] = jnp.zeros((SUBLANES, dr), F32)
        hcar[...] = jnp.zeros_like(hcar)

    xbuf[SUBLANES:SUBLANES + tm, :] = xr_ref[...]
    xc = cb_ref[...] + jnp.zeros((tm, dr), F32)
    for j in range(cwid):
        off = SUBLANES - (cwid - 1) + j
        xc = xc + cw_ref[j:j + 1, :] * xbuf[off:off + tm, :]
    xbuf[0:SUBLANES, :] = xbuf[tm:tm + SUBLANES, :]
    _rglru_gates(xc, wg_ref, ba_ref, bx_ref, lam_ref, a_scr, b_scr)

    def body(g, carry):
        rows = pl.ds(pl.multiple_of(g * SUBLANES, SUBLANES), SUBLANES)
        a, b = _scan8(a_scr[rows, :], b_scr[rows, :])
        h = a * carry + b
        ya_ref[rows, :] = h * gg_ref[rows, :]
        return h[SUBLANES - 1:SUBLANES, :]

    hc = lax.fori_loop(0, tm // SUBLANES, body, hcar[...])
    hcar[...] = hc
    hlast_ref[...] = hc


def _rglru_prompt_call(xr, gg, cw, cb, wg, ba, bx, lam, tm):
    t, dr = xr.shape
    row = pl.BlockSpec((tm, dr), lambda i: (i, 0))
    vec = _resident((1, dr))
    return pl.pallas_call(
        _rglru_prompt_kernel,
        out_shape=[jax.ShapeDtypeStruct((t, dr), F32), jax.ShapeDtypeStruct((1, dr), F32)],
        grid=(t // tm,),
        in_specs=[row, row, _resident(cw.shape), vec, _resident(wg.shape), vec, vec, vec],
        out_specs=[row, pl.BlockSpec((1, dr), lambda i: (0, 0))],
        scratch_shapes=[pltpu.VMEM((tm + SUBLANES, dr), F32), pltpu.VMEM((tm, dr), F32),
                        pltpu.VMEM((tm, dr), F32), pltpu.VMEM((1, dr), F32)],
        compiler_params=_cparams(("arbitrary",)),
        name="rglru_prompt",
    )(xr, gg, cw, cb.reshape(1, dr), wg, ba.reshape(1, dr), bx.reshape(1, dr), lam.reshape(1, dr))


def _rglru_sample_kernel(xcat_ref, gg_ref, h0_ref, cw_ref, cb_ref, wg_ref, ba_ref, bx_ref, lam_ref, ya_ref, h_ref,
                         xc_scr, a_scr, b_scr):
    nseq, _, dr = xcat_ref.shape
    cwid = cw_ref.shape[0]

    def conv(s, _):
        xc = cb_ref[...] + jnp.zeros((SUBLANES, dr), F32)
        for j in range(cwid):
            xc = xc + cw_ref[j:j + 1, :] * xcat_ref[s, j:j + SUBLANES, :]
        xc_scr[pl.ds(pl.multiple_of(s * SUBLANES, SUBLANES), SUBLANES), :] = xc
        return 0

    lax.fori_loop(0, nseq, conv, 0)
    _rglru_gates(xc_scr[...], wg_ref, ba_ref, bx_ref, lam_ref, a_scr, b_scr)

    def body(s, _):
        rows = pl.ds(pl.multiple_of(s * SUBLANES, SUBLANES), SUBLANES)
        a, b = _scan8(a_scr[rows, :], b_scr[rows, :])
        h = a * h0_ref[s] + b
        h_ref[rows, :] = h
        ya_ref[rows, :] = h * gg_ref[rows, :]
        return 0

    lax.fori_loop(0, nseq, body, 0)


def _rglru_sample_call(xcat, gg, h0, cw, cb, wg, ba, bx, lam):
    nseq, _, dr = xcat.shape
    t = nseq * SUBLANES
    full = lambda shape: pl.BlockSpec(shape, lambda i: (0,) * len(shape))
    return pl.pallas_call(
        _rglru_sample_kernel,
        out_shape=[jax.ShapeDtypeStruct((t, dr), F32), jax.ShapeDtypeStruct((t, dr), F32)],
        grid=(1,),
        in_specs=[full(xcat.shape), full((t, dr)), full((nseq, 1, dr)), full(cw.shape), full((1, dr)),
                  full(wg.shape), full((1, dr)), full((1, dr)), full((1, dr))],
        out_specs=[full((t, dr)), full((t, dr))],
        scratch_shapes=[pltpu.VMEM((t, dr), F32), pltpu.VMEM((t, dr), F32), pltpu.VMEM((t, dr), F32)],
        compiler_params=_cparams(("arbitrary",)),
        name="rglru_sample",
    )(xcat, gg, h0.reshape(nseq, 1, dr), cw, cb.reshape(1, dr), wg, ba.reshape(1, dr), bx.reshape(1, dr),
      lam.reshape(1, dr))


def _select_blocks_cols(gs, n_eligible):
    nb = gs.shape[0]
    jj = lax.broadcasted_iota(jnp.int32, gs.shape, 0)
    s = jnp.where(jj < n_eligible, gs, -jnp.inf)
    sel = jnp.zeros(gs.shape, F32)
    for p in range(MOBA_TOP_K):
        m = jnp.max(s, axis=0, keepdims=True)
        idx = jnp.min(jnp.where(s == m, jj, nb), axis=0, keepdims=True)
        hit = jj == idx
        sel = jnp.where(jnp.logical_and(hit, jj * 0 + p < n_eligible), 1.0, sel)
        s = jnp.where(hit, -jnp.inf, s)
    return sel


def _select_blocks_rows(gs, n_eligible):
    nb = gs.shape[1]
    jj = lax.broadcasted_iota(jnp.int32, gs.shape, 1)
    s = jnp.where(jj < n_eligible, gs, -jnp.inf)
    sel = jnp.zeros(gs.shape, F32)
    for p in range(MOBA_TOP_K):
        m = jnp.max(s, axis=1, keepdims=True)
        idx = jnp.min(jnp.where(s == m, jj, nb), axis=1, keepdims=True)
        hit = jj == idx
        sel = jnp.where(jnp.logical_and(hit, jj * 0 + p < n_eligible), 1.0, sel)
        s = jnp.where(hit, -jnp.inf, s)
    return sel


def _attn_prompt_kernel(q_ref, k_ref, vt_ref, km_ref, bias_ref, cfar_ref, o_ref, sel_scr, m_scr, l_scr, acc_scr):
    i = pl.program_id(1)
    tq, hd2 = q_ref.shape
    hd = hd2 // 2
    blk = MOBA_BLOCK
    q2 = q_ref[...]
    qt = jnp.concatenate([q2[:, :hd].T, q2[:, hd:].T], axis=1)
    qtb = qt.astype(BF16)
    sel_scr[...] = _select_blocks_cols(_dot_f32(km_ref[...], qt), i)

    def kv(j):
        off = pl.multiple_of(j * blk, blk)
        return k_ref[pl.ds(off, blk), :], vt_ref[:, pl.ds(off, blk)]

    k_own, vt_own = kv(i)
    st = _dot(k_own, qtb) + bias_ref[1]
    m = jnp.max(st, axis=0, keepdims=True)
    p = jnp.exp(st - m)
    m_scr[...] = m
    l_scr[...] = jnp.sum(p, axis=0, keepdims=True)
    acc_scr[...] = _dot(vt_own, p.astype(BF16))

    def past_step(j, st):
        _, vt_j = kv(j)
        m_old = m_scr[...]
        m_new = jnp.maximum(m_old, jnp.max(st, axis=0, keepdims=True))
        alpha = jnp.exp(m_old - m_new)
        p = jnp.exp(st - m_new)
        m_scr[...] = m_new
        l_scr[...] = alpha * l_scr[...] + jnp.sum(p, axis=0, keepdims=True)
        acc_scr[...] = alpha * acc_scr[...] + _dot(vt_j, p.astype(BF16))

    def far_body(j, _):
        k_j, _ = kv(j)
        row = jnp.where(sel_scr[pl.ds(j, 1), :] > 0.5, cfar_ref[...], NEG)
        past_step(j, _dot(k_j, qtb) + row)
        return 0

    lax.fori_loop(0, jnp.maximum(i - 1, 0), far_body, 0)

    @pl.when(i >= 1)
    def _():
        j = i - 1
        k_j, _ = kv(j)
        row = jnp.where(sel_scr[pl.ds(j, 1), :] > 0.5, 0.0, NEG)
        past_step(j, _dot(k_j, qtb) + bias_ref[0] + row)

    ot = acc_scr[...] / l_scr[...]
    o_ref[:, :hd] = ot[:, :tq].T
    o_ref[:, hd:] = ot[:, tq:].T


def _attn_prompt_call(q, kb, vt, km, bias, cfar, nkv, hd):
    s = q.shape[0]
    tq = MOBA_BLOCK
    nb = s // tq
    return pl.pallas_call(
        _attn_prompt_kernel,
        out_shape=jax.ShapeDtypeStruct(q.shape, F32),
        grid=(nkv, nb),
        in_specs=[
            pl.BlockSpec((tq, 2 * hd), lambda h, i: (i, h)),
            pl.BlockSpec((s, hd), lambda h, i: (0, h)),
            pl.BlockSpec((hd, s), lambda h, i: (h, 0)),
            pl.BlockSpec((nb, hd), lambda h, i: (0, h)),
            pl.BlockSpec((None, 2, tq, 2 * tq), lambda h, i: (h, 0, 0, 0)),
            pl.BlockSpec((None, 1, 2 * tq), lambda h, i: (h, 0, 0)),
        ],
        out_specs=pl.BlockSpec((tq, 2 * hd), lambda h, i: (i, h)),
        scratch_shapes=[pltpu.VMEM((nb, 2 * tq), F32), pltpu.VMEM((1, 2 * tq), F32), pltpu.VMEM((1, 2 * tq), F32),
                        pltpu.VMEM((hd, 2 * tq), F32)],
        compiler_params=_cparams(("arbitrary", "arbitrary")),
        name="moba_prompt",
    )(q, kb, vt, km, bias, cfar)


def _kmean_sample_kernel(pt_ref, *refs, ppb):
    pages, o_ref = refs[:-1], refs[-1]
    npg = len(pages)
    psz = pages[0].shape[0]
    for b in range(npg // ppb):
        tot = jnp.sum(pages[b * ppb][...], axis=0, keepdims=True)
        for r in range(1, ppb):
            tot = tot + jnp.sum(pages[b * ppb + r][...], axis=0, keepdims=True)
        o_ref[b:b + 1, :] = tot * (1.0 / (ppb * psz))


def _paged(layer, p, npg):
    def index(b, g, pt):
        return (layer, pt[b, g * npg + p], 0, 0)
    return index


def _kmean_sample_call(cache_k, page_table, layer, nb):
    _, _, psz, dkv = cache_k.shape
    bs, npages = page_table.shape
    ppb = MOBA_BLOCK // psz
    npg = PAGES_PER_STEP
    bps = npg // ppb
    grid_spec = pltpu.PrefetchScalarGridSpec(
        num_scalar_prefetch=1,
        grid=(bs, nb * ppb // npg),
        in_specs=[pl.BlockSpec((None, None, psz, dkv), _paged(layer, p, npg)) for p in range(npg)],
        out_specs=pl.BlockSpec((None, bps, dkv), lambda b, g, pt: (b, g, 0)),
    )
    return pl.pallas_call(
        functools.partial(_kmean_sample_kernel, ppb=ppb),
        out_shape=jax.ShapeDtypeStruct((bs, nb, dkv), F32),
        grid_spec=grid_spec,
        compiler_params=_cparams(("arbitrary", "arbitrary")),
        name="kmean_sample",
    )(page_table, *([cache_k] * npg))


def _attn_sample_kernel(pt_ref, q_ref, kn_ref, vn_ref, km_ref, bias_ref, bown_ref, *refs, nkv, hd, ppb):
    npg = (len(refs) - 5) // 2
    kpages, vpages = refs[:npg], refs[npg:2 * npg]
    o_ref, sel_scr, m_scr, l_scr, acc_scr = refs[2 * npg:]
    g = pl.program_id(1)
    ng = pl.num_programs(1)
    lq = q_ref.shape[0]
    rows = 2 * lq
    nb = km_ref.shape[0]
    psz = kpages[0].shape[0]

    def qpair(h):
        q = q_ref[...]
        return jnp.concatenate([q[:, (2 * h) * hd:(2 * h + 1) * hd], q[:, (2 * h + 1) * hd:(2 * h + 2) * hd]], axis=0)

    @pl.when(g == 0)
    def _():
        for h in range(nkv):
            qp = qpair(h)
            cols = slice(h * hd, (h + 1) * hd)
            sel_scr[h] = _select_blocks_rows(_dot_nt_f32(qp, km_ref[:, cols]), nb)
            pad = jnp.zeros((psz - lq, hd), F32)
            k_own = jnp.concatenate([kn_ref[:, cols], pad], axis=0).astype(BF16)
            v_own = jnp.concatenate([vn_ref[:, cols], pad], axis=0).astype(BF16)
            s = _dot_nt(qp.astype(BF16), k_own) + bown_ref[h]
            m = jnp.max(s, axis=1, keepdims=True)
            p = jnp.exp(s - m)
            m_scr[h] = m
            l_scr[h] = jnp.sum(p, axis=1, keepdims=True)
            acc_scr[h] = _dot(p.astype(BF16), v_own)

    lane = lax.broadcasted_iota(jnp.int32, (rows, nb), 1)
    for h in range(nkv):
        qpb = qpair(h).astype(BF16)
        cols = slice(h * hd, (h + 1) * hd)
        sel_h = sel_scr[h]
        for pg in range(npg):
            j = (g * npg + pg) // ppb
            picked = jnp.sum(jnp.where(lane == j, sel_h, 0.0), axis=1, keepdims=True)
            s = _dot_nt(qpb, kpages[pg][:, cols].astype(BF16)) + bias_ref[h, pg] + jnp.where(picked > 0.5, 0.0, NEG)
            m_old = m_scr[h]
            m_new = jnp.maximum(m_old, jnp.max(s, axis=1, keepdims=True))
            alpha = jnp.exp(m_old - m_new)
            p = jnp.exp(s - m_new)
            m_scr[h] = m_new
            l_scr[h] = alpha * l_scr[h] + jnp.sum(p, axis=1, keepdims=True)
            acc_scr[h] = alpha * acc_scr[h] + _dot(p.astype(BF16), vpages[pg][:, cols].astype(BF16))

    @pl.when(g == ng - 1)
    def _():
        for h in range(nkv):
            o = acc_scr[h] / l_scr[h]
            o_ref[:, (2 * h) * hd:(2 * h + 1) * hd] = o[:lq]
            o_ref[:, (2 * h + 1) * hd:(2 * h + 2) * hd] = o[lq:]


def _attn_sample_call(q, kn, vn, km, bias, bown, cache_k, cache_v, page_table, layer, lq, nkv, hd):
    _, _, psz, dkv = cache_k.shape
    bs, npages = page_table.shape
    nb = km.shape[1]
    ppb = MOBA_BLOCK // psz
    npg = PAGES_PER_STEP
    ng = npages // npg
    dq = q.shape[1]
    rows = 2 * lq
    page_spec = lambda p: pl.BlockSpec((None, None, psz, dkv), _paged(layer, p, npg))
    grid_spec = pltpu.PrefetchScalarGridSpec(
        num_scalar_prefetch=1,
        grid=(bs, ng),
        in_specs=[
            pl.BlockSpec((lq, dq), lambda b, g, pt: (b, 0)),
            pl.BlockSpec((lq, dkv), lambda b, g, pt: (b, 0)),
            pl.BlockSpec((lq, dkv), lambda b, g, pt: (b, 0)),
            pl.BlockSpec((None, nb, dkv), lambda b, g, pt: (b, 0, 0)),
            pl.BlockSpec((nkv, npg, rows, psz), lambda b, g, pt: (0, g, 0, 0)),
            pl.BlockSpec((nkv, rows, psz), lambda b, g, pt: (0, 0, 0)),
        ] + [page_spec(p) for p in range(npg)] * 2,
        out_specs=pl.BlockSpec((lq, dq), lambda b, g, pt: (b, 0)),
        scratch_shapes=[pltpu.VMEM((nkv, rows, nb), F32), pltpu.VMEM((nkv, rows, 1), F32),
                        pltpu.VMEM((nkv, rows, 1), F32), pltpu.VMEM((nkv, rows, hd), F32)],
    )
    return pl.pallas_call(
        functools.partial(_attn_sample_kernel, nkv=nkv, hd=hd, ppb=ppb),
        out_shape=jax.ShapeDtypeStruct(q.shape, F32),
        grid_spec=grid_spec,
        compiler_params=_cparams(("arbitrary", "arbitrary")),
        name="moba_sample",
    )(page_table, q, kn, vn, km, bias, bown, *([cache_k] * npg), *([cache_v] * npg))


def _rel_bucket(dist, n_buckets):
    n = jnp.maximum(dist, 0)
    max_exact = n_buckets // 2
    nf = jnp.maximum(n, 1).astype(F32)
    large = max_exact + (jnp.log(nf / max_exact) / math.log(MAX_DISTANCE / max_exact)
                         * (n_buckets - max_exact)).astype(jnp.int32)
    large = jnp.minimum(large, n_buckets - 1)
    return jnp.where(n < max_exact, n, large)


def _bias_by_distance(dist, rel_table):
    b = rel_table[_rel_bucket(dist, rel_table.shape[0])]
    return jnp.where((dist >= 0)[..., None], b, NEG)


def _prompt_bias_tiles(rel_table, nkv):
    blk = MOBA_BLOCK
    kk = jnp.arange(blk)[:, None]
    qq = jnp.arange(blk)[None, :]
    tiles = jnp.stack([_bias_by_distance(qq + blk - kk, rel_table), _bias_by_distance(qq - kk, rel_table)])
    tiles = tiles.reshape(2, blk, blk, nkv, 2)
    tiles = jnp.transpose(tiles, (3, 0, 1, 4, 2)).reshape(nkv, 2, blk, 2 * blk)
    far = _bias_by_distance(jnp.full((1,), 2 * blk), rel_table).reshape(nkv, 2, 1)
    cfar = jnp.broadcast_to(far, (nkv, 2, blk)).reshape(nkv, 1, 2 * blk)
    return tiles, cfar


def _sample_bias_tiles(rel_table, nkv, lq, past_len, psz):
    npages = past_len // psz
    qpos = past_len + jnp.arange(lq)
    kpos = jnp.arange(past_len).reshape(npages, 1, psz)
    past = _bias_by_distance(qpos[None, :, None] - kpos, rel_table)
    past = past.reshape(npages, lq, psz, nkv, 2)
    past = jnp.transpose(past, (3, 0, 4, 1, 2)).reshape(nkv, npages, 2 * lq, psz)
    own_k = past_len + jnp.arange(psz)
    own = _bias_by_distance(qpos[:, None] - own_k[None, :], rel_table)
    own = jnp.where((jnp.arange(psz) < lq)[None, :, None], own, NEG)
    own = jnp.transpose(own.reshape(lq, psz, nkv, 2), (2, 3, 0, 1)).reshape(nkv, 2 * lq, psz)
    return past, own


def _block_diag_gates(w_a, w_x, group):
    n, c, _ = w_a.shape
    eye = jnp.eye(group, dtype=w_a.dtype)

    def bd(w):
        w = w.reshape(n // group, group, c, c)
        return jnp.einsum('gbcd,be->gbced', w, eye).reshape(n // group, group * c, group * c)

    return jnp.concatenate([bd(w_a), bd(w_x)], axis=-1).astype(BF16)


def kernel(x_prompt, x_sample, c_prompt, c_sample, cache_k, cache_v, page_table, state_conv, state_rglru, norm_g,
           w_ada, b_ada, w_in, conv_w, conv_b, w_rg_a, b_rg_a, w_rg_x, b_rg_x, lru_lambda, q_norm_g, k_norm_g,
           rel_table, w_out, w_ffn_in, w_ffn_out):
    depth = norm_g.shape[0]
    bp, s, d = x_prompt.shape
    bs, lq, _ = x_sample.shape
    _, npool, psz, nkv, hd = cache_k.shape
    nh = rel_table.shape[1]
    dr = conv_w.shape[2]
    cwid = conv_w.shape[1]
    past_len = page_table.shape[1] * psz
    nb_past = past_len // MOBA_BLOCK
    assert bp == 1 and s % MOBA_BLOCK == 0 and lq == SUBLANES and nh == 2 * nkv
    assert past_len == nb_past * MOBA_BLOCK and MOBA_BLOCK % psz == 0 and cwid - 1 <= lq
    dkv = nkv * hd
    tm_p = 512 if s % 512 == 0 else MOBA_BLOCK
    ts = bs * lq

    c_all = jnp.concatenate([c_prompt, c_sample], axis=0)
    nrow = -(-c_all.shape[0] // SUBLANES) * SUBLANES
    c_all = jnp.pad(c_all, ((0, nrow - c_all.shape[0]), (0, 0)))
    mods = _ada_call(c_all, w_ada, b_ada)

    ck = cache_k.reshape(depth, npool, psz, dkv)
    cv = cache_v.reshape(depth, npool, psz, dkv)
    bias_p, cfar_p = _prompt_bias_tiles(rel_table, nkv)
    bias_s, bown_s = _sample_bias_tiles(rel_table, nkv, lq, past_len, psz)

    xp = x_prompt.reshape(s, d)
    xs = x_sample.reshape(ts, d)
    outs = {k: [] for k in ('kp', 'vp', 'cp', 'hp', 'ks', 'vs', 'cs', 'hs')}
    for l in range(depth):
        mod_p = mods[l, 0:bp]
        mod_s = jnp.repeat(mods[l, bp:bp + bs], lq, axis=0)
        wup0, wup1 = w_ffn_in[l, 0].astype(BF16), w_ffn_in[l, 1].astype(BF16)
        wdn0, wdn1 = w_ffn_out[l, 0].astype(BF16), w_ffn_out[l, 1].astype(BF16)
        win = w_in[l].astype(BF16)
        wout = w_out[l].astype(BF16)
        wg = _block_diag_gates(w_rg_a[l], w_rg_x[l], 4)
        rg = (conv_w[l], conv_b[l], wg, b_rg_a[l], b_rg_x[l], lru_lambda[l])
        proj = dict(dr=dr, nh=nh, nkv=nkv, hd=hd)

        xp = _ffn_call(xp, mod_p, 0, norm_g[l, 0], wup0, wdn0, tm_p)
        xr, gg, sgb, q, k, v, kb, vt, km = _inproj_call(xp, mod_p, norm_g[l, 1], win, q_norm_g[l], k_norm_g[l], tm_p,
                                                        prompt=True, **proj)
        ya, hlast = _rglru_prompt_call(xr, gg, *rg, tm_p)
        yatt = _attn_prompt_call(q, kb, vt, km.reshape(s // MOBA_BLOCK, dkv), bias_p, cfar_p, nkv, hd)
        xp = _outproj_ffn_call(xp, ya, sgb, yatt, wout, mod_p, 1, 2, norm_g[l, 2], wup1, wdn1, tm_p)
        outs['kp'].append(k.reshape(bp, s, nkv, hd))
        outs['vp'].append(v.reshape(bp, s, nkv, hd))
        outs['cp'].append(xr[s - (cwid - 1):].reshape(bp, cwid - 1, dr))
        outs['hp'].append(hlast)

        xs = _ffn_call(xs, mod_s, 0, norm_g[l, 0], wup0, wdn0, ts)
        xr, gg, sgb, q, k, v = _inproj_call(xs, mod_s, norm_g[l, 1], win, q_norm_g[l], k_norm_g[l], ts,
                                            prompt=False, **proj)
        xr3 = xr.reshape(bs, lq, dr)
        xcat = jnp.concatenate([state_conv[l], xr3, jnp.zeros((bs, 2 * SUBLANES - lq - (cwid - 1), dr), F32)], axis=1)
        ya, h_all = _rglru_sample_call(xcat, gg, state_rglru[l], *rg)
        km_s = _kmean_sample_call(ck, page_table, l, nb_past)
        yatt = _attn_sample_call(q, k, v, km_s, bias_s, bown_s, ck, cv, page_table, l, lq, nkv, hd)
        xs = _outproj_ffn_call(xs, ya, sgb, yatt, wout, mod_s, 1, 2, norm_g[l, 2], wup1, wdn1, ts)
        outs['ks'].append(k.reshape(bs, lq, nkv, hd))
        outs['vs'].append(v.reshape(bs, lq, nkv, hd))
        outs['cs'].append(xr3[:, lq - (cwid - 1):])
        outs['hs'].append(h_all.reshape(bs, lq, dr)[:, lq - 1])

    st = {k: jnp.stack(v) for k, v in outs.items()}
    return (xp.reshape(bp, s, d), xs.reshape(bs, lq, d), st['kp'], st['vp'], st['cp'], st['hp'],
            st['ks'], st['vs'], st['cs'], st['hs'])
```

```python
import functools
import math

import jax
import jax.numpy as jnp
from jax import lax
from jax.experimental import pallas as pl
from jax.experimental.pallas import tpu as pltpu

F32 = jnp.float32
BF16 = jnp.bfloat16

EPS = 1e-6
LRU_C = 8.0
MOBA_BLOCK = 256
MOBA_TOP_K = 3
MAX_DISTANCE = 128
NEG = -1e30
LOG2E = 1.4426950408889634

V7X_VMEM_BYTES = 64 * 1024 * 1024
VMEM_LIMIT = V7X_VMEM_BYTES * 3 // 4
SUBLANES = 8
LANES = 128
BF16_ROWS = 16
PAGES_PER_STEP = 16
FAR_BLOCKS = 4


def _cparams(sem):
    return pltpu.CompilerParams(dimension_semantics=sem, vmem_limit_bytes=VMEM_LIMIT)


def _resident(shape):
    nd = len(shape)
    return pl.BlockSpec(shape, lambda *_: (0,) * nd, pipeline_mode=pl.Buffered(1))


def _sigmoid(x):
    return 1.0 / (1.0 + jnp.exp(-x))


def _gelu_tanh(x):
    return 0.5 * x * (1.0 + jnp.tanh(math.sqrt(2.0 / math.pi) * (x + 0.044715 * (x * x * x))))


def _rms_mod(x, g, shift, scale):
    ms = jnp.mean(x * x, axis=-1, keepdims=True)
    y = x * lax.rsqrt(ms + EPS) * g
    return y * (1.0 + scale) + shift


def _dot(a, b):
    return jnp.dot(a, b, preferred_element_type=F32)


def _dot_nt(a, b):
    return lax.dot_general(a, b, (((1,), (1,)), ((), ())), preferred_element_type=F32)


def _dot_f32(a, b):
    return jnp.dot(a, b, preferred_element_type=F32, precision=lax.Precision.HIGHEST)


def _dot_nt_f32(a, b):
    return lax.dot_general(a, b, (((1,), (1,)), ((), ())), preferred_element_type=F32,
                           precision=lax.Precision.HIGHEST)


def _ada_kernel(c_ref, w_ref, b_ref, o_ref):
    c = c_ref[...]
    o_ref[...] = _dot_f32(c * _sigmoid(c), w_ref[...]) + b_ref[...]


def _ada_call(c_all, w_ada, b_ada):
    depth, d, n = w_ada.shape
    rows = c_all.shape[0]
    tn = 1024
    return pl.pallas_call(
        _ada_kernel,
        out_shape=jax.ShapeDtypeStruct((depth, rows, n), F32),
        grid=(depth, n // tn),
        in_specs=[
            pl.BlockSpec((rows, d), lambda l, j: (0, 0)),
            pl.BlockSpec((None, d, tn), lambda l, j: (l, 0, j)),
            pl.BlockSpec((None, 1, tn), lambda l, j: (l, 0, j)),
        ],
        out_specs=pl.BlockSpec((None, rows, tn), lambda l, j: (l, 0, j)),
        compiler_params=_cparams(("arbitrary", "arbitrary")),
        name="adaln",
    )(c_all, w_ada, b_ada.reshape(depth, 1, n))


def _mod_spec(mod, tm, sub, d):
    if mod.shape[0] == 1:
        return pl.BlockSpec((1, 3 * d), lambda i: (0, sub))
    return pl.BlockSpec((tm, 3 * d), lambda i: (i, sub))


def _ffn_body(x, mod_ref, g_ref, wup_ref, wdn_ref, fc):
    d = x.shape[-1]
    f = wdn_ref.shape[0]
    h = _rms_mod(x, g_ref[...], mod_ref[:, 0:d], mod_ref[:, d:2 * d]).astype(BF16)
    acc = jnp.zeros(x.shape, F32)
    for c in range(f // fc):
        a = _dot(h, wup_ref[:, c * fc:(c + 1) * fc])
        g = _dot(h, wup_ref[:, f + c * fc:f + (c + 1) * fc])
        act = (g * _sigmoid(g) * a).astype(BF16)
        acc = acc + _dot(act, wdn_ref[c * fc:(c + 1) * fc, :])
    return x + 0.5 * mod_ref[:, 2 * d:3 * d] * acc


def _ffn_kernel(x_ref, mod_ref, g_ref, wup_ref, wdn_ref, o_ref, *, fc):
    o_ref[...] = _ffn_body(x_ref[...], mod_ref, g_ref, wup_ref, wdn_ref, fc)


def _outproj_ffn_kernel(x_ref, ya_ref, sgb_ref, yatt_ref, wout_ref, modo_ref, mod_ref, g_ref, wup_ref, wdn_ref,
                        o_ref, *, fc):
    x = x_ref[...]
    d = x.shape[-1]
    merged = (ya_ref[...] + sgb_ref[...] * yatt_ref[...]).astype(BF16)
    x = x + modo_ref[:, 2 * d:3 * d] * _dot(merged, wout_ref[...])
    o_ref[...] = _ffn_body(x, mod_ref, g_ref, wup_ref, wdn_ref, fc)


def _ffn_chunk(f):
    for fc in (512, 256, 128):
        if f % fc == 0:
            return fc
    return f


def _ffn_call(x, mod, sub, g, wup, wdn, tm):
    t, d = x.shape
    row = pl.BlockSpec((tm, d), lambda i: (i, 0))
    return pl.pallas_call(
        functools.partial(_ffn_kernel, fc=_ffn_chunk(wdn.shape[0])),
        out_shape=jax.ShapeDtypeStruct((t, d), F32),
        grid=(t // tm,),
        in_specs=[row, _mod_spec(mod, tm, sub, d), _resident((1, d)), _resident(wup.shape), _resident(wdn.shape)],
        out_specs=row,
        compiler_params=_cparams(("parallel",)),
        name="ffn",
    )(x, mod, g.reshape(1, d), wup, wdn)


def _outproj_ffn_call(x, ya, sgb, yatt, wout, mod, sub_out, sub, g, wup, wdn, tm):
    t, d = x.shape
    row = pl.BlockSpec((tm, d), lambda i: (i, 0))
    return pl.pallas_call(
        functools.partial(_outproj_ffn_kernel, fc=_ffn_chunk(wdn.shape[0])),
        out_shape=jax.ShapeDtypeStruct((t, d), F32),
        grid=(t // tm,),
        in_specs=[row, row, row, row, _resident(wout.shape), _mod_spec(mod, tm, sub_out, d),
                  _mod_spec(mod, tm, sub, d), _resident((1, d)), _resident(wup.shape), _resident(wdn.shape)],
        out_specs=row,
        compiler_params=_cparams(("parallel",)),
        name="outproj_ffn",
    )(x, ya, sgb, yatt, wout, mod, mod, g.reshape(1, d), wup, wdn)


def _head_rms(u, g, nheads, hd, scale):
    outs = []
    for h in range(nheads):
        uh = u[:, h * hd:(h + 1) * hd]
        ms = jnp.mean(uh * uh, axis=-1, keepdims=True)
        outs.append(uh * lax.rsqrt(ms + EPS) * g * scale)
    return outs


def _inproj_kernel(x_ref, mod_ref, g_ref, win_ref, qg_ref, kg_ref, *out_refs, dr, nh, nkv, hd, prompt):
    if prompt:
        xr_ref, gg_ref, sgb_ref, q_ref, k_ref, v_ref, kb_ref, vt_ref, km_ref = out_refs
    else:
        xr_ref, gg_ref, sgb_ref, q_ref, k_ref, v_ref = out_refs
    x = x_ref[...]
    tm, d = x.shape
    h = _rms_mod(x, g_ref[...], mod_ref[:, 0:d], mod_ref[:, d:2 * d]).astype(BF16)
    dq, dkv = nh * hd, nkv * hd
    o_q = 2 * dr
    o_k = o_q + dq
    o_v = o_k + dkv
    o_ga = o_v + dkv
    o_gb = o_ga + d

    def seg(lo, width):
        return _dot(h, win_ref[:, lo:lo + width])

    xr_ref[...] = seg(0, dr)
    gg_ref[...] = _sigmoid(seg(o_ga, d)) * _gelu_tanh(seg(dr, dr))
    sgb_ref[...] = _sigmoid(seg(o_gb, d))
    qs = _head_rms(seg(o_q, dq), qg_ref[...], nh, hd, hd ** -0.5)
    for hh in range(nh):
        q_ref[:, hh * hd:(hh + 1) * hd] = qs[hh]
    ks = _head_rms(seg(o_k, dkv), kg_ref[...], nkv, hd, 1.0)
    v = seg(o_v, dkv)
    v_ref[...] = v
    for hh in range(nkv):
        k_ref[:, hh * hd:(hh + 1) * hd] = ks[hh]
    if prompt:
        vt = v.T.astype(BF16)
        hv = hd + BF16_ROWS
        for hh in range(nkv):
            vt_ref[hh * hv:hh * hv + hd, :] = vt[hh * hd:(hh + 1) * hd, :]
            vt_ref[hh * hv + hd:(hh + 1) * hv, :] = jnp.ones((BF16_ROWS, tm), BF16)
            kb_ref[:, hh * hd:(hh + 1) * hd] = ks[hh].astype(BF16)
            for b in range(tm // MOBA_BLOCK):
                km_ref[b, :, hh * hd:(hh + 1) * hd] = jnp.mean(
                    ks[hh][b * MOBA_BLOCK:(b + 1) * MOBA_BLOCK], axis=0, keepdims=True)


def _inproj_call(x, mod, g, win, qg, kg, tm, *, dr, nh, nkv, hd, prompt):
    t, d = x.shape
    dq, dkv = nh * hd, nkv * hd
    row = lambda w: pl.BlockSpec((tm, w), lambda i: (i, 0))
    out_shape = [jax.ShapeDtypeStruct((t, w), F32) for w in (dr, d, d, dq, dkv, dkv)]
    out_specs = [row(w) for w in (dr, d, d, dq, dkv, dkv)]
    if prompt:
        nbt = tm // MOBA_BLOCK
        dvt = nkv * (hd + BF16_ROWS)
        out_shape += [jax.ShapeDtypeStruct((t, dkv), BF16), jax.ShapeDtypeStruct((dvt, t), BF16),
                      jax.ShapeDtypeStruct((t // MOBA_BLOCK, 1, dkv), F32)]
        out_specs += [row(dkv), pl.BlockSpec((dvt, tm), lambda i: (0, i)),
                      pl.BlockSpec((nbt, 1, dkv), lambda i: (i, 0, 0))]
    return pl.pallas_call(
        functools.partial(_inproj_kernel, dr=dr, nh=nh, nkv=nkv, hd=hd, prompt=prompt),
        out_shape=out_shape,
        grid=(t // tm,),
        in_specs=[row(d), _mod_spec(mod, tm, 1, d), _resident((1, d)), _resident(win.shape),
                  _resident((1, hd)), _resident((1, hd))],
        out_specs=out_specs,
        compiler_params=_cparams(("parallel",)),
        name="inproj",
    )(x, mod, g.reshape(1, d), win, qg.reshape(1, hd), kg.reshape(1, hd))


def _rglru_gates(xc, wg_ref, ba_ref, bx_ref, lam_ref, a_scr, b_scr):
    ngroups, gw, _ = wg_ref.shape
    xcb = xc.astype(BF16)
    for gi in range(ngroups):
        cols = slice(gi * gw, (gi + 1) * gw)
        u = _dot(xcb[:, cols], wg_ref[gi])
        r = _sigmoid(u[:, :gw] + ba_ref[:, cols])
        gate_x = _sigmoid(u[:, gw:] + bx_ref[:, cols])
        lam = lam_ref[:, cols]
        softplus_neg = jnp.maximum(-lam, 0.0) + jnp.log(1.0 + jnp.exp(-jnp.abs(lam)))
        log_a = -LRU_C * r * softplus_neg
        a = jnp.exp(log_a)
        mult = jnp.sqrt(-_expm1(2.0 * log_a))
        a_scr[:, cols] = a
        b_scr[:, cols] = mult * gate_x * xc[:, cols]


def _expm1(x):
    small = x * (1.0 + x * (0.5 + x * (1.0 / 6.0 + x * (1.0 / 24.0 + x * (1.0 / 120.0)))))
    return jnp.where(x > -0.1, small, jnp.exp(x) - 1.0)


def _scan8(a, b):
    row = lax.broadcasted_iota(jnp.int32, a.shape, 0)
    for k in (1, 2, 4):
        keep = row >= k
        a_sh = jnp.where(keep, pltpu.roll(a, k, 0), 1.0)
        b_sh = jnp.where(keep, pltpu.roll(b, k, 0), 0.0)
        b = a * b_sh + b
        a = a * a_sh
    return a, b


def _rglru_prompt_kernel(xr_ref, gg_ref, cw_ref, cb_ref, wg_ref, ba_ref, bx_ref, lam_ref, ya_ref, hlast_ref,
                         xbuf, a_scr, b_scr, hcar):
    i = pl.program_id(0)
    tm, dr = xr_ref.shape
    cwid = cw_ref.shape[0]

    @pl.when(i == 0)
    def _():
        xbuf[0:SUBLANES, :] = jnp.zeros((SUBLANES, dr), F32)
        hcar[...] = jnp.zeros_like(hcar)

    xbuf[SUBLANES:SUBLANES + tm, :] = xr_ref[...]
    xc = cb_ref[...] + jnp.zeros((tm, dr), F32)
    for j in range(cwid):
        off = SUBLANES - (cwid - 1) + j
        xc = xc + cw_ref[j:j + 1, :] * xbuf[off:off + tm, :]
    xbuf[0:SUBLANES, :] = xbuf[tm:tm + SUBLANES, :]
    _rglru_gates(xc, wg_ref, ba_ref, bx_ref, lam_ref, a_scr, b_scr)

    def body(g, carry):
        rows = pl.ds(pl.multiple_of(g * SUBLANES, SUBLANES), SUBLANES)
        a, b = _scan8(a_scr[rows, :], b_scr[rows, :])
        h = a * carry + b
        ya_ref[rows, :] = h * gg_ref[rows, :]
        return h[SUBLANES - 1:SUBLANES, :]

    hc = lax.fori_loop(0, tm // SUBLANES, body, hcar[...])
    hcar[...] = hc
    hlast_ref[...] = hc


def _rglru_prompt_call(xr, gg, cw, cb, wg, ba, bx, lam, tm):
    t, dr = xr.shape
    row = pl.BlockSpec((tm, dr), lambda i: (i, 0))
    vec = _resident((1, dr))
    return pl.pallas_call(
        _rglru_prompt_kernel,
        out_shape=[jax.ShapeDtypeStruct((t, dr), F32), jax.ShapeDtypeStruct((1, dr), F32)],
        grid=(t // tm,),
        in_specs=[row, row, _resident(cw.shape), vec, _resident(wg.shape), vec, vec, vec],
        out_specs=[row, pl.BlockSpec((1, dr), lambda i: (0, 0))],
        scratch_shapes=[pltpu.VMEM((tm + SUBLANES, dr), F32), pltpu.VMEM((tm, dr), F32),
                        pltpu.VMEM((tm, dr), F32), pltpu.VMEM((1, dr), F32)],
        compiler_params=_cparams(("arbitrary",)),
        name="rglru_prompt",
    )(xr, gg, cw, cb.reshape(1, dr), wg, ba.reshape(1, dr), bx.reshape(1, dr), lam.reshape(1, dr))


def _rglru_sample_kernel(xcat_ref, gg_ref, h0_ref, cw_ref, cb_ref, wg_ref, ba_ref, bx_ref, lam_ref, ya_ref, h_ref,
                         xc_scr, a_scr, b_scr):
    nseq, _, dr = xcat_ref.shape
    cwid = cw_ref.shape[0]

    def conv(s, _):
        xc = cb_ref[...] + jnp.zeros((SUBLANES, dr), F32)
        for j in range(cwid):
            xc = xc + cw_ref[j:j + 1, :] * xcat_ref[s, j:j + SUBLANES, :]
        xc_scr[pl.ds(pl.multiple_of(s * SUBLANES, SUBLANES), SUBLANES), :] = xc
        return 0

    lax.fori_loop(0, nseq, conv, 0)
    _rglru_gates(xc_scr[...], wg_ref, ba_ref, bx_ref, lam_ref, a_scr, b_scr)

    def body(s, _):
        rows = pl.ds(pl.multiple_of(s * SUBLANES, SUBLANES), SUBLANES)
        a, b = _scan8(a_scr[rows, :], b_scr[rows, :])
        h = a * h0_ref[s] + b
        h_ref[rows, :] = h
        ya_ref[rows, :] = h * gg_ref[rows, :]
        return 0

    lax.fori_loop(0, nseq, body, 0)


def _rglru_sample_call(xcat, gg, h0, cw, cb, wg, ba, bx, lam):
    nseq, _, dr = xcat.shape
    t = nseq * SUBLANES
    full = lambda shape: pl.BlockSpec(shape, lambda i: (0,) * len(shape))
    return pl.pallas_call(
        _rglru_sample_kernel,
        out_shape=[jax.ShapeDtypeStruct((t, dr), F32), jax.ShapeDtypeStruct((t, dr), F32)],
        grid=(1,),
        in_specs=[full(xcat.shape), full((t, dr)), full((nseq, 1, dr)), full(cw.shape), full((1, dr)),
                  full(wg.shape), full((1, dr)), full((1, dr)), full((1, dr))],
        out_specs=[full((t, dr)), full((t, dr))],
        scratch_shapes=[pltpu.VMEM((t, dr), F32), pltpu.VMEM((t, dr), F32), pltpu.VMEM((t, dr), F32)],
        compiler_params=_cparams(("arbitrary",)),
        name="rglru_sample",
    )(xcat, gg, h0.reshape(nseq, 1, dr), cw, cb.reshape(1, dr), wg, ba.reshape(1, dr), bx.reshape(1, dr),
      lam.reshape(1, dr))


def _select_blocks(gs, n_eligible, axis):
    nb = gs.shape[axis]
    jj = lax.broadcasted_iota(jnp.int32, gs.shape, axis)
    s = jnp.where(jj < n_eligible, gs, -jnp.inf)
    sel = jnp.zeros(gs.shape, F32)
    for p in range(MOBA_TOP_K):
        m = jnp.max(s, axis=axis, keepdims=True)
        idx = jnp.min(jnp.where(s == m, jj, nb), axis=axis, keepdims=True)
        hit = jj == idx
        sel = jnp.where(jnp.logical_and(hit, jj * 0 + p < n_eligible), 1.0, sel)
        s = jnp.where(hit, -jnp.inf, s)
    return sel


def _attn_prompt_kernel(q_ref, k_ref, vt_ref, km_ref, near_ref, e_ref, place_ref, o_ref,
                        qa_scr, fm_scr, m_scr, acc_scr):
    i = pl.program_id(1)
    tq, hd2 = q_ref.shape
    hd = hd2 // 2
    blk = MOBA_BLOCK
    cb = FAR_BLOCKS * blk
    nb = km_ref.shape[0]
    q2 = q_ref[...]
    qt = jnp.concatenate([q2[:, :hd].T, q2[:, hd:].T], axis=1)
    sel = _select_blocks(_dot_f32(km_ref[...], qt), i, 0)
    jj = lax.broadcasted_iota(jnp.int32, sel.shape, 0)
    adj_sel = jnp.max(jnp.where(jj == i - 1, sel, 0.0), axis=0, keepdims=True)
    far_sel = jnp.where(jj < i - 1, sel, 0.0).astype(BF16)
    placed = _dot(place_ref[...], far_sel)
    prow = lax.broadcasted_iota(jnp.int32, placed.shape, 0)
    fm = jnp.where(prow % BF16_ROWS < FAR_BLOCKS, jnp.where(placed > 0.5, 0.0, NEG), 0.0)
    fm_scr[...] = fm.astype(BF16)
    qa_scr[0:hd, :] = (qt * LOG2E).astype(BF16)
    qa_scr[hd + BF16_ROWS:2 * hd, :] = jnp.zeros((hd - BF16_ROWS, 2 * tq), BF16)
    qtb = qa_scr[0:hd, :]

    ja = pl.multiple_of(jnp.maximum(i - 1, 0) * blk, blk)
    jo = pl.multiple_of(i * blk, blk)
    s_adj = _dot(k_ref[pl.ds(ja, blk), :], qtb) + (near_ref[0] + jnp.where(adj_sel > 0.5, 0.0, NEG))
    s_own = _dot(k_ref[pl.ds(jo, blk), :], qtb) + near_ref[1]
    m = jnp.maximum(jnp.max(s_adj, axis=0, keepdims=True), jnp.max(s_own, axis=0, keepdims=True))
    p_adj = jnp.exp2(s_adj - m).astype(BF16)
    p_own = jnp.exp2(s_own - m).astype(BF16)
    acc_scr[...] = _dot(vt_ref[:, pl.ds(ja, blk)], p_adj) + _dot(vt_ref[:, pl.ds(jo, blk)], p_own)
    m_scr[...] = m

    def far_body(c, _):
        off = pl.multiple_of(c * cb, cb)
        qa_scr[hd:hd + BF16_ROWS, :] = fm_scr[pl.ds(pl.multiple_of(c * BF16_ROWS, BF16_ROWS), BF16_ROWS), :]
        kc = jnp.concatenate([k_ref[pl.ds(off, cb), :], e_ref[...]], axis=1)
        st = _dot(kc, qa_scr[...])
        m_old = m_scr[...]
        m_new = jnp.maximum(m_old, jnp.max(st, axis=0, keepdims=True))
        p = jnp.exp2(st - m_new).astype(BF16)
        acc_scr[...] = jnp.exp2(m_old - m_new) * acc_scr[...] + _dot(vt_ref[:, pl.ds(off, cb)], p)
        m_scr[...] = m_new
        return 0

    lax.fori_loop(0, jnp.maximum(i + FAR_BLOCKS - 2, 0) // FAR_BLOCKS, far_body, 0)

    acc = acc_scr[...]
    ot = acc[:hd] / acc[hd:hd + 1]
    o_ref[:, :hd] = ot[:, :tq].T
    o_ref[:, hd:] = ot[:, tq:].T


def _attn_prompt_call(q, kb, vt, km, near, nkv, hd):
    s = q.shape[0]
    tq = MOBA_BLOCK
    nb = s // tq
    assert nb % FAR_BLOCKS == 0 and nb <= hd
    cb = FAR_BLOCKS * tq
    nchunk = nb // FAR_BLOCKS
    hv = hd + BF16_ROWS
    e = (jnp.arange(hd)[None, :] == (jnp.arange(cb) // tq)[:, None]).astype(BF16)
    pr = jnp.arange(nchunk * BF16_ROWS)
    place = ((pr % BF16_ROWS < FAR_BLOCKS)[:, None]
             & ((pr // BF16_ROWS * FAR_BLOCKS + pr % BF16_ROWS)[:, None] == jnp.arange(nb)[None, :])).astype(BF16)
    return pl.pallas_call(
        _attn_prompt_kernel,
        out_shape=jax.ShapeDtypeStruct(q.shape, F32),
        grid=(nkv, nb),
        in_specs=[
            pl.BlockSpec((tq, 2 * hd), lambda h, i: (i, h)),
            pl.BlockSpec((s, hd), lambda h, i: (0, h)),
            pl.BlockSpec((hv, s), lambda h, i: (h, 0)),
            pl.BlockSpec((nb, hd), lambda h, i: (0, h)),
            pl.BlockSpec((None, 2, tq, 2 * tq), lambda h, i: (h, 0, 0, 0)),
            _resident(e.shape),
            _resident(place.shape),
        ],
        out_specs=pl.BlockSpec((tq, 2 * hd), lambda h, i: (i, h)),
        scratch_shapes=[pltpu.VMEM((2 * hd, 2 * tq), BF16), pltpu.VMEM((nchunk * BF16_ROWS, 2 * tq), BF16),
                        pltpu.VMEM((1, 2 * tq), F32), pltpu.VMEM((hv, 2 * tq), F32)],
        compiler_params=_cparams(("arbitrary", "arbitrary")),
        name="moba_prompt",
    )(q, kb, vt, km, near, e, place)


def _paged(layer, p, npg):
    def index(b, g, pt):
        return (layer, pt[b, g * npg + p], 0, 0)
    return index


def _kmean_sample_kernel(pt_ref, *refs, ppb, nkv):
    pages, o_ref = refs[:-1], refs[-1]
    rows, hd = pages[0].shape
    for b in range(len(pages) // ppb):
        tot = jnp.zeros((SUBLANES, hd), F32)
        for r in range(ppb):
            tot = tot + jnp.sum(pages[b * ppb + r][...].reshape(rows // SUBLANES, SUBLANES, hd), axis=0)
        tot = tot + pltpu.roll(tot, nkv, 0)
        o_ref[b] = tot * (1.0 / (ppb * (rows // nkv)))


def _kmean_sample_call(cache, page_table, layer, nb, nkv):
    _, _, prow, hd = cache.shape
    bs, npages = page_table.shape
    ppb = npages // nb
    npg = PAGES_PER_STEP
    assert 2 * nkv == SUBLANES and npg % ppb == 0 and npages % npg == 0
    bps = npg // ppb
    grid_spec = pltpu.PrefetchScalarGridSpec(
        num_scalar_prefetch=1,
        grid=(bs, npages // npg),
        in_specs=[pl.BlockSpec((None, None, prow, hd), _paged(layer, p, npg)) for p in range(npg)],
        out_specs=pl.BlockSpec((None, bps, SUBLANES, hd), lambda b, g, pt: (b, g, 0, 0)),
    )
    return pl.pallas_call(
        functools.partial(_kmean_sample_kernel, ppb=ppb, nkv=nkv),
        out_shape=jax.ShapeDtypeStruct((bs, nb, SUBLANES, hd), F32),
        grid_spec=grid_spec,
        compiler_params=_cparams(("arbitrary", "arbitrary")),
        name="kmean_sample",
    )(page_table, *([cache] * npg))


def _attn_sample_kernel(pt_ref, q_ref, kn_ref, vn_ref, km_ref, base_ref, last_ref, own_ref, *refs, nkv, ppb):
    npg = (len(refs) - 4) // 2
    kpages, vpages = refs[:npg], refs[npg:2 * npg]
    o_ref, sel_scr, m_scr, acc_scr = refs[2 * npg:]
    g = pl.program_id(1)
    ng = pl.num_programs(1)
    rows, hd = q_ref.shape
    rph = rows // nkv
    nb = km_ref.shape[1]
    prow = kpages[0].shape[0]
    q = q_ref[...]
    qb = (q * LOG2E).astype(BF16)
    ones = jnp.ones((prow, hd), BF16)

    def pv(p, v):
        return _dot(p.astype(BF16), jnp.concatenate([v.astype(BF16), ones[:v.shape[0]]], axis=1))

    @pl.when(g == 0)
    def _():
        for h in range(nkv):
            rs = slice(h * rph, (h + 1) * rph)
            sel_scr[rs, :] = _select_blocks(_dot_nt_f32(q[rs], km_ref[h]), nb, 1)
        pad = jnp.zeros((LANES - kn_ref.shape[0], hd), F32)
        s = _dot_nt(qb, jnp.concatenate([kn_ref[...], pad], axis=0).astype(BF16)) + own_ref[...]
        m = jnp.max(s, axis=1, keepdims=True)
        m_scr[...] = m
        acc_scr[...] = pv(jnp.exp2(s - m), jnp.concatenate([vn_ref[...], pad], axis=0))

    lane = lax.broadcasted_iota(jnp.int32, (rows, nb), 1)
    sel = sel_scr[...]
    is_last = g == ng - 1
    ss = []
    for pg in range(npg):
        if pg % ppb == 0:
            j = (g * npg + pg) // ppb
            picked = jnp.sum(jnp.where(lane == j, sel, 0.0), axis=1, keepdims=True)
            selcol = jnp.where(picked > 0.5, 0.0, NEG)
        tile = base_ref[...]
        if pg >= npg - ppb:
            tile = jnp.where(is_last, last_ref[pg - (npg - ppb)], tile)
        ss.append(_dot_nt(qb, kpages[pg][...].astype(BF16)) + (tile + selcol))
    mx = ss[0]
    for s in ss[1:]:
        mx = jnp.maximum(mx, s)
    m_old = m_scr[...]
    m_new = jnp.maximum(m_old, jnp.max(mx, axis=1, keepdims=True))
    acc = jnp.exp2(m_old - m_new) * acc_scr[...]
    for pg in range(npg):
        acc = acc + pv(jnp.exp2(ss[pg] - m_new), vpages[pg][...])
    m_scr[...] = m_new
    acc_scr[...] = acc

    @pl.when(is_last)
    def _():
        o_ref[...] = acc[:, :hd] / acc[:, hd:hd + 1]


def _attn_sample_call(qall, kn, vn, km, base, last, own, ck, cv, page_table, layer, nkv):
    _, _, prow, hd = ck.shape
    bs, npages = page_table.shape
    nb = km.shape[2]
    ppb = npages // nb
    npg = PAGES_PER_STEP
    rows = qall.shape[1]
    nown = kn.shape[1]
    assert nown <= LANES
    page_spec = lambda p: pl.BlockSpec((None, None, prow, hd), _paged(layer, p, npg))
    const = lambda shape: pl.BlockSpec(shape, lambda b, g, pt: (0,) * len(shape))
    grid_spec = pltpu.PrefetchScalarGridSpec(
        num_scalar_prefetch=1,
        grid=(bs, npages // npg),
        in_specs=[
            pl.BlockSpec((None, rows, hd), lambda b, g, pt: (b, 0, 0)),
            pl.BlockSpec((None, nown, hd), lambda b, g, pt: (b, 0, 0)),
            pl.BlockSpec((None, nown, hd), lambda b, g, pt: (b, 0, 0)),
            pl.BlockSpec((None, nkv, nb, hd), lambda b, g, pt: (b, 0, 0, 0)),
            const(base.shape), const(last.shape), const(own.shape),
        ] + [page_spec(p) for p in range(npg)] * 2,
        out_specs=pl.BlockSpec((None, rows, hd), lambda b, g, pt: (b, 0, 0)),
        scratch_shapes=[pltpu.VMEM((rows, nb), F32), pltpu.VMEM((rows, 1), F32), pltpu.VMEM((rows, 2 * hd), F32)],
    )
    return pl.pallas_call(
        functools.partial(_attn_sample_kernel, nkv=nkv, ppb=ppb),
        out_shape=jax.ShapeDtypeStruct(qall.shape, F32),
        grid_spec=grid_spec,
        compiler_params=_cparams(("arbitrary", "arbitrary")),
        name="moba_sample",
    )(page_table, qall, kn, vn, km, base, last, own, *([ck] * npg), *([cv] * npg))


def _rel_bucket(dist, n_buckets):
    n = jnp.maximum(dist, 0)
    max_exact = n_buckets // 2
    nf = jnp.maximum(n, 1).astype(F32)
    large = max_exact + (jnp.log(nf / max_exact) / math.log(MAX_DISTANCE / max_exact)
                         * (n_buckets - max_exact)).astype(jnp.int32)
    large = jnp.minimum(large, n_buckets - 1)
    return jnp.where(n < max_exact, n, large)


def _shifted_bias(dist, rel_table):
    nbk = rel_table.shape[0]
    far = rel_table[_rel_bucket(jnp.asarray(2 * MOBA_BLOCK), nbk)]
    b = (rel_table[_rel_bucket(dist, nbk)] - far) * LOG2E
    return jnp.where((dist >= 0)[..., None], b, NEG)


def _skew_toeplitz(u, n):
    lead = u.shape[:-1]
    up = jnp.concatenate([u, jnp.zeros(lead + (1,), u.dtype)], axis=-1)
    flat = jnp.broadcast_to(up[..., None, :], lead + (n, 2 * n)).reshape(lead + (2 * n * n,))
    return flat[..., :n * (2 * n - 1)].reshape(lead + (n, 2 * n - 1))


def _prompt_near_tiles(rel_table, nkv):
    blk = MOBA_BLOCK
    n = 2 * blk
    u = _shifted_bias(jnp.arange(2 * n - 1) - (blk - 1), rel_table).T
    a = _skew_toeplitz(u, n)[:, :, n - 1:n - 1 + blk]
    a = a.reshape(nkv, 2, 2, blk, blk)
    return jnp.transpose(a, (0, 2, 3, 1, 4)).reshape(nkv, 2, blk, 2 * blk)


def _sample_tiles(rel_table, nkv, lq, psz, ppb):
    rows = 2 * nkv * lq
    row_kvh = jnp.arange(rows) // (2 * lq)
    row_t = jnp.arange(rows) % lq
    col_h = jnp.arange(psz * nkv) % nkv
    base = jnp.where(row_kvh[:, None] == col_h[None, :], 0.0, NEG).astype(F32)
    kk = jnp.arange(ppb * psz)
    b = _shifted_bias(jnp.arange(lq)[:, None] + ppb * psz - kk[None, :], rel_table)
    b = jnp.transpose(b, (2, 0, 1)).reshape(rows, ppb, psz)
    last = jnp.transpose(b, (1, 0, 2))[:, :, :, None] + base.reshape(1, rows, psz, nkv)
    last = last.reshape(ppb, rows, psz * nkv)
    ncol = LANES
    col_t, col_hh = jnp.arange(ncol) // nkv, jnp.arange(ncol) % nkv
    ob = _shifted_bias(jnp.arange(lq)[:, None] - col_t[None, :], rel_table)
    ob = jnp.transpose(ob, (2, 0, 1)).reshape(rows, ncol)
    ok = (row_kvh[:, None] == col_hh[None, :]) & (col_t[None, :] < lq)
    own = jnp.where(ok, ob, NEG)
    return base, last, own


def _block_diag_gates(w_a, w_x, group):
    n, c, _ = w_a.shape
    eye = jnp.eye(group, dtype=w_a.dtype)

    def bd(w):
        w = w.reshape(n // group, group, c, c)
        return jnp.einsum('gbcd,be->gbced', w, eye).reshape(n // group, group * c, group * c)

    return jnp.concatenate([bd(w_a), bd(w_x)], axis=-1).astype(BF16)


def kernel(x_prompt, x_sample, c_prompt, c_sample, cache_k, cache_v, page_table, state_conv, state_rglru, norm_g,
           w_ada, b_ada, w_in, conv_w, conv_b, w_rg_a, b_rg_a, w_rg_x, b_rg_x, lru_lambda, q_norm_g, k_norm_g,
           rel_table, w_out, w_ffn_in, w_ffn_out):
    depth = norm_g.shape[0]
    bp, s, d = x_prompt.shape
    bs, lq, _ = x_sample.shape
    _, npool, psz, nkv, hd = cache_k.shape
    nh = rel_table.shape[1]
    dr = conv_w.shape[2]
    cwid = conv_w.shape[1]
    past_len = page_table.shape[1] * psz
    nb_past = past_len // MOBA_BLOCK
    assert bp == 1 and s % MOBA_BLOCK == 0 and lq == SUBLANES and nh == 2 * nkv
    assert past_len == nb_past * MOBA_BLOCK and MOBA_BLOCK % psz == 0 and cwid - 1 <= lq
    assert MAX_DISTANCE <= MOBA_BLOCK
    dkv = nkv * hd
    ppb = MOBA_BLOCK // psz
    tm_p = 512 if s % 512 == 0 else MOBA_BLOCK
    ts = bs * lq

    c_all = jnp.concatenate([c_prompt, c_sample], axis=0)
    nrow = -(-c_all.shape[0] // SUBLANES) * SUBLANES
    c_all = jnp.pad(c_all, ((0, nrow - c_all.shape[0]), (0, 0)))
    mods = _ada_call(c_all, w_ada, b_ada)

    ck = cache_k.reshape(depth, npool, psz * nkv, hd)
    cv = cache_v.reshape(depth, npool, psz * nkv, hd)
    near_p = _prompt_near_tiles(rel_table, nkv)
    base_s, last_s, own_s = _sample_tiles(rel_table, nkv, lq, psz, ppb)

    xp = x_prompt.reshape(s, d)
    xs = x_sample.reshape(ts, d)
    outs = {k: [] for k in ('kp', 'vp', 'cp', 'hp', 'ks', 'vs', 'cs', 'hs')}
    for l in range(depth):
        mod_p = mods[l, 0:bp]
        mod_s = jnp.repeat(mods[l, bp:bp + bs], lq, axis=0)
        wup0, wup1 = w_ffn_in[l, 0].astype(BF16), w_ffn_in[l, 1].astype(BF16)
        wdn0, wdn1 = w_ffn_out[l, 0].astype(BF16), w_ffn_out[l, 1].astype(BF16)
        win = w_in[l].astype(BF16)
        wout = w_out[l].astype(BF16)
        wg = _block_diag_gates(w_rg_a[l], w_rg_x[l], 4)
        rg = (conv_w[l], conv_b[l], wg, b_rg_a[l], b_rg_x[l], lru_lambda[l])
        proj = dict(dr=dr, nh=nh, nkv=nkv, hd=hd)

        xp = _ffn_call(xp, mod_p, 0, norm_g[l, 0], wup0, wdn0, tm_p)
        xr, gg, sgb, q, k, v, kb, vt, km = _inproj_call(xp, mod_p, norm_g[l, 1], win, q_norm_g[l], k_norm_g[l], tm_p,
                                                        prompt=True, **proj)
        ya, hlast = _rglru_prompt_call(xr, gg, *rg, tm_p)
        yatt = _attn_prompt_call(q, kb, vt, km.reshape(s // MOBA_BLOCK, dkv), near_p, nkv, hd)
        xp = _outproj_ffn_call(xp, ya, sgb, yatt, wout, mod_p, 1, 2, norm_g[l, 2], wup1, wdn1, tm_p)
        outs['kp'].append(k.reshape(bp, s, nkv, hd))
        outs['vp'].append(v.reshape(bp, s, nkv, hd))
        outs['cp'].append(xr[s - (cwid - 1):].reshape(bp, cwid - 1, dr))
        outs['hp'].append(hlast)

        xs = _ffn_call(xs, mod_s, 0, norm_g[l, 0], wup0, wdn0, ts)
        xr, gg, sgb, q, k, v = _inproj_call(xs, mod_s, norm_g[l, 1], win, q_norm_g[l], k_norm_g[l], ts,
                                            prompt=False, **proj)
        xr3 = xr.reshape(bs, lq, dr)
        xcat = jnp.concatenate([state_conv[l], xr3, jnp.zeros((bs, 2 * SUBLANES - lq - (cwid - 1), dr), F32)], axis=1)
        ya, h_all = _rglru_sample_call(xcat, gg, state_rglru[l], *rg)
        km_s = _kmean_sample_call(ck, page_table, l, nb_past, nkv)
        km_s = jnp.transpose(km_s[:, :, :nkv], (0, 2, 1, 3))
        qall = jnp.transpose(q.reshape(bs, lq, nkv, 2, hd), (0, 2, 3, 1, 4)).reshape(bs, 2 * nkv * lq, hd)
        yatt = _attn_sample_call(qall, k.reshape(bs, lq * nkv, hd), v.reshape(bs, lq * nkv, hd), km_s,
                                 base_s, last_s, own_s, ck, cv, page_table, l, nkv)
        yatt = jnp.transpose(yatt.reshape(bs, nkv, 2, lq, hd), (0, 3, 1, 2, 4)).reshape(ts, nh * hd)
        xs = _outproj_ffn_call(xs, ya, sgb, yatt, wout, mod_s, 1, 2, norm_g[l, 2], wup1, wdn1, ts)
        outs['ks'].append(k.reshape(bs, lq, nkv, hd))
        outs['vs'].append(v.reshape(bs, lq, nkv, hd))
        outs['cs'].append(xr3[:, lq - (cwid - 1):])
        outs['hs'].append(h_all.reshape(bs, lq, dr)[:, lq - 1])

    st = {k: jnp.stack(v) for k, v in outs.items()}
    return (xp.reshape(bp, s, d), xs.reshape(bs, lq, d), st['kp'], st['vp'], st['cp'], st['hp'],
            st['ks'], st['vs'], st['cs'], st['hs'])
```

```python
import functools
import math

import jax
import jax.numpy as jnp
from jax import lax
from jax.experimental import pallas as pl
from jax.experimental.pallas import tpu as pltpu

F32 = jnp.float32
BF16 = jnp.bfloat16

EPS = 1e-6
LRU_C = 8.0
MOBA_BLOCK = 256
MOBA_TOP_K = 3
MAX_DISTANCE = 128
NEG = -1e30
LOG2E = 1.4426950408889634

V7X_VMEM_BYTES = 64 * 1024 * 1024
VMEM_LIMIT = V7X_VMEM_BYTES * 3 // 4
SUBLANES = 8
LANES = 128
BF16_ROWS = 16
PAGES_PER_STEP = 16
FAR_BLOCKS = 4


def _cparams(sem):
    return pltpu.CompilerParams(dimension_semantics=sem, vmem_limit_bytes=VMEM_LIMIT)


def _resident(shape):
    nd = len(shape)
    return pl.BlockSpec(shape, lambda *_: (0,) * nd, pipeline_mode=pl.Buffered(1))


def _sigmoid(x):
    return 1.0 / (1.0 + jnp.exp(-x))


def _gelu_tanh(x):
    return 0.5 * x * (1.0 + jnp.tanh(math.sqrt(2.0 / math.pi) * (x + 0.044715 * (x * x * x))))


def _rms_mod(x, g, shift, scale):
    ms = jnp.mean(x * x, axis=-1, keepdims=True)
    y = x * lax.rsqrt(ms + EPS) * g
    return y * (1.0 + scale) + shift


def _dot(a, b):
    return jnp.dot(a, b, preferred_element_type=F32)


def _dot_nt(a, b):
    return lax.dot_general(a, b, (((1,), (1,)), ((), ())), preferred_element_type=F32)


def _dot_f32(a, b):
    return jnp.dot(a, b, preferred_element_type=F32, precision=lax.Precision.HIGHEST)


def _dot_nt_f32(a, b):
    return lax.dot_general(a, b, (((1,), (1,)), ((), ())), preferred_element_type=F32,
                           precision=lax.Precision.HIGHEST)


def _ada_kernel(c_ref, w_ref, b_ref, o_ref):
    c = c_ref[...]
    o_ref[...] = _dot_f32(c * _sigmoid(c), w_ref[...]) + b_ref[...]


def _ada_call(c_all, w_ada, b_ada):
    depth, d, n = w_ada.shape
    rows = c_all.shape[0]
    tn = 1024
    return pl.pallas_call(
        _ada_kernel,
        out_shape=jax.ShapeDtypeStruct((depth, rows, n), F32),
        grid=(depth, n // tn),
        in_specs=[
            pl.BlockSpec((rows, d), lambda l, j: (0, 0)),
            pl.BlockSpec((None, d, tn), lambda l, j: (l, 0, j)),
            pl.BlockSpec((None, 1, tn), lambda l, j: (l, 0, j)),
        ],
        out_specs=pl.BlockSpec((None, rows, tn), lambda l, j: (l, 0, j)),
        compiler_params=_cparams(("arbitrary", "arbitrary")),
        name="adaln",
    )(c_all, w_ada, b_ada.reshape(depth, 1, n))


def _mod_spec(mod, tm, sub, d):
    if mod.shape[0] == 1:
        return pl.BlockSpec((1, 3 * d), lambda i: (0, sub))
    return pl.BlockSpec((tm, 3 * d), lambda i: (i, sub))


def _ffn_body(x, mod_ref, g_ref, wup_ref, wdn_ref, fc):
    d = x.shape[-1]
    f = wdn_ref.shape[0]
    h = _rms_mod(x, g_ref[...], mod_ref[:, 0:d], mod_ref[:, d:2 * d]).astype(BF16)
    acc = jnp.zeros(x.shape, F32)
    for c in range(f // fc):
        a = _dot(h, wup_ref[:, c * fc:(c + 1) * fc])
        g = _dot(h, wup_ref[:, f + c * fc:f + (c + 1) * fc])
        act = (g * _sigmoid(g) * a).astype(BF16)
        acc = acc + _dot(act, wdn_ref[c * fc:(c + 1) * fc, :])
    return x + 0.5 * mod_ref[:, 2 * d:3 * d] * acc


def _ffn_kernel(x_ref, mod_ref, g_ref, wup_ref, wdn_ref, o_ref, *, fc):
    o_ref[...] = _ffn_body(x_ref[...], mod_ref, g_ref, wup_ref, wdn_ref, fc)


def _outproj_ffn_kernel(x_ref, ya_ref, sgb_ref, yatt_ref, wout_ref, modo_ref, mod_ref, g_ref, wup_ref, wdn_ref,
                        o_ref, *, fc):
    x = x_ref[...]
    d = x.shape[-1]
    merged = (ya_ref[...] + sgb_ref[...] * yatt_ref[...]).astype(BF16)
    x = x + modo_ref[:, 2 * d:3 * d] * _dot(merged, wout_ref[...])
    o_ref[...] = _ffn_body(x, mod_ref, g_ref, wup_ref, wdn_ref, fc)


def _ffn_chunk(f):
    for fc in (512, 256, 128):
        if f % fc == 0:
            return fc
    return f


def _ffn_call(x, mod, sub, g, wup, wdn, tm):
    t, d = x.shape
    row = pl.BlockSpec((tm, d), lambda i: (i, 0))
    return pl.pallas_call(
        functools.partial(_ffn_kernel, fc=_ffn_chunk(wdn.shape[0])),
        out_shape=jax.ShapeDtypeStruct((t, d), F32),
        grid=(t // tm,),
        in_specs=[row, _mod_spec(mod, tm, sub, d), _resident((1, d)), _resident(wup.shape), _resident(wdn.shape)],
        out_specs=row,
        compiler_params=_cparams(("parallel",)),
        name="ffn",
    )(x, mod, g.reshape(1, d), wup, wdn)


def _outproj_ffn_call(x, ya, sgb, yatt, wout, mod, sub_out, sub, g, wup, wdn, tm):
    t, d = x.shape
    row = pl.BlockSpec((tm, d), lambda i: (i, 0))
    return pl.pallas_call(
        functools.partial(_outproj_ffn_kernel, fc=_ffn_chunk(wdn.shape[0])),
        out_shape=jax.ShapeDtypeStruct((t, d), F32),
        grid=(t // tm,),
        in_specs=[row, row, row, row, _resident(wout.shape), _mod_spec(mod, tm, sub_out, d),
                  _mod_spec(mod, tm, sub, d), _resident((1, d)), _resident(wup.shape), _resident(wdn.shape)],
        out_specs=row,
        compiler_params=_cparams(("parallel",)),
        name="outproj_ffn",
    )(x, ya, sgb, yatt, wout, mod, mod, g.reshape(1, d), wup, wdn)


def _head_rms(u, g, nheads, hd, scale):
    outs = []
    for h in range(nheads):
        uh = u[:, h * hd:(h + 1) * hd]
        ms = jnp.mean(uh * uh, axis=-1, keepdims=True)
        outs.append(uh * lax.rsqrt(ms + EPS) * g * scale)
    return outs


def _inproj_kernel(x_ref, mod_ref, g_ref, win_ref, qg_ref, kg_ref, *out_refs, dr, nh, nkv, hd, prompt):
    if prompt:
        xr_ref, gg_ref, sgb_ref, q_ref, k_ref, v_ref, kb_ref, vt_ref, km_ref = out_refs
    else:
        xr_ref, gg_ref, sgb_ref, q_ref, k_ref, v_ref = out_refs
    x = x_ref[...]
    tm, d = x.shape
    h = _rms_mod(x, g_ref[...], mod_ref[:, 0:d], mod_ref[:, d:2 * d]).astype(BF16)
    dq, dkv = nh * hd, nkv * hd
    o_q = 2 * dr
    o_k = o_q + dq
    o_v = o_k + dkv
    o_ga = o_v + dkv
    o_gb = o_ga + d

    def seg(lo, width):
        return _dot(h, win_ref[:, lo:lo + width])

    xr_ref[...] = seg(0, dr)
    gg_ref[...] = _sigmoid(seg(o_ga, d)) * _gelu_tanh(seg(dr, dr))
    sgb_ref[...] = _sigmoid(seg(o_gb, d))
    qs = _head_rms(seg(o_q, dq), qg_ref[...], nh, hd, hd ** -0.5)
    for hh in range(nh):
        q_ref[:, hh * hd:(hh + 1) * hd] = qs[hh]
    ks = _head_rms(seg(o_k, dkv), kg_ref[...], nkv, hd, 1.0)
    v = seg(o_v, dkv)
    v_ref[...] = v
    for hh in range(nkv):
        k_ref[:, hh * hd:(hh + 1) * hd] = ks[hh]
    if prompt:
        vt = v.T.astype(BF16)
        hv = hd + BF16_ROWS
        for hh in range(nkv):
            vt_ref[hh * hv:hh * hv + hd, :] = vt[hh * hd:(hh + 1) * hd, :]
            vt_ref[hh * hv + hd:(hh + 1) * hv, :] = jnp.ones((BF16_ROWS, tm), BF16)
            kb_ref[:, hh * hd:(hh + 1) * hd] = ks[hh].astype(BF16)
            for b in range(tm // MOBA_BLOCK):
                km_ref[b, :, hh * hd:(hh + 1) * hd] = jnp.mean(
                    ks[hh][b * MOBA_BLOCK:(b + 1) * MOBA_BLOCK], axis=0, keepdims=True)


def _inproj_call(x, mod, g, win, qg, kg, tm, *, dr, nh, nkv, hd, prompt):
    t, d = x.shape
    dq, dkv = nh * hd, nkv * hd
    row = lambda w: pl.BlockSpec((tm, w), lambda i: (i, 0))
    out_shape = [jax.ShapeDtypeStruct((t, w), F32) for w in (dr, d, d, dq, dkv, dkv)]
    out_specs = [row(w) for w in (dr, d, d, dq, dkv, dkv)]
    if prompt:
        nbt = tm // MOBA_BLOCK
        dvt = nkv * (hd + BF16_ROWS)
        out_shape += [jax.ShapeDtypeStruct((t, dkv), BF16), jax.ShapeDtypeStruct((dvt, t), BF16),
                      jax.ShapeDtypeStruct((t // MOBA_BLOCK, 1, dkv), F32)]
        out_specs += [row(dkv), pl.BlockSpec((dvt, tm), lambda i: (0, i)),
                      pl.BlockSpec((nbt, 1, dkv), lambda i: (i, 0, 0))]
    return pl.pallas_call(
        functools.partial(_inproj_kernel, dr=dr, nh=nh, nkv=nkv, hd=hd, prompt=prompt),
        out_shape=out_shape,
        grid=(t // tm,),
        in_specs=[row(d), _mod_spec(mod, tm, 1, d), _resident((1, d)), _resident(win.shape),
                  _resident((1, hd)), _resident((1, hd))],
        out_specs=out_specs,
        compiler_params=_cparams(("parallel",)),
        name="inproj",
    )(x, mod, g.reshape(1, d), win, qg.reshape(1, hd), kg.reshape(1, hd))


def _rglru_gates(xc, wg_ref, ba_ref, bx_ref, lam_ref, a_scr, b_scr):
    ngroups, gw, _ = wg_ref.shape
    xcb = xc.astype(BF16)
    for gi in range(ngroups):
        cols = slice(gi * gw, (gi + 1) * gw)
        u = _dot(xcb[:, cols], wg_ref[gi])
        r = _sigmoid(u[:, :gw] + ba_ref[:, cols])
        gate_x = _sigmoid(u[:, gw:] + bx_ref[:, cols])
        lam = lam_ref[:, cols]
        softplus_neg = jnp.maximum(-lam, 0.0) + jnp.log(1.0 + jnp.exp(-jnp.abs(lam)))
        log_a = -LRU_C * r * softplus_neg
        a = jnp.exp(log_a)
        mult = jnp.sqrt(-_expm1(2.0 * log_a))
        a_scr[:, cols] = a
        b_scr[:, cols] = mult * gate_x * xc[:, cols]


def _expm1(x):
    small = x * (1.0 + x * (0.5 + x * (1.0 / 6.0 + x * (1.0 / 24.0 + x * (1.0 / 120.0)))))
    return jnp.where(x > -0.1, small, jnp.exp(x) - 1.0)


def _scan8(a, b):
    row = lax.broadcasted_iota(jnp.int32, a.shape, 0)
    for k in (1, 2, 4):
        keep = row >= k
        a_sh = jnp.where(keep, pltpu.roll(a, k, 0), 1.0)
        b_sh = jnp.where(keep, pltpu.roll(b, k, 0), 0.0)
        b = a * b_sh + b
        a = a * a_sh
    return a, b


def _rglru_prompt_kernel(xr_ref, gg_ref, cw_ref, cb_ref, wg_ref, ba_ref, bx_ref, lam_ref, ya_ref, hlast_ref,
                         xbuf, a_scr, b_scr, hcar):
    i = pl.program_id(0)
    tm, dr = xr_ref.shape
    cwid = cw_ref.shape[0]

    @pl.when(i == 0)
    def _():
        xbuf[0:SUBLANES, :] = jnp.zeros((SUBLANES, dr), F32)
        hcar[...] = jnp.zeros_like(hcar)

    xbuf[SUBLANES:SUBLANES + tm, :] = xr_ref[...]
    xc = cb_ref[...] + jnp.zeros((tm, dr), F32)
    for j in range(cwid):
        off = SUBLANES - (cwid - 1) + j
        xc = xc + cw_ref[j:j + 1, :] * xbuf[off:off + tm, :]
    xbuf[0:SUBLANES, :] = xbuf[tm:tm + SUBLANES, :]
    _rglru_gates(xc, wg_ref, ba_ref, bx_ref, lam_ref, a_scr, b_scr)

    def body(g, carry):
        rows = pl.ds(pl.multiple_of(g * SUBLANES, SUBLANES), SUBLANES)
        a, b = _scan8(a_scr[rows, :], b_scr[rows, :])
        h = a * carry + b
        ya_ref[rows, :] = h * gg_ref[rows, :]
        return h[SUBLANES - 1:SUBLANES, :]

    hc = lax.fori_loop(0, tm // SUBLANES, body, hcar[...])
    hcar[...] = hc
    hlast_ref[...] = hc


def _rglru_prompt_call(xr, gg, cw, cb, wg, ba, bx, lam, tm):
    t, dr = xr.shape
    row = pl.BlockSpec((tm, dr), lambda i: (i, 0))
    vec = _resident((1, dr))
    return pl.pallas_call(
        _rglru_prompt_kernel,
        out_shape=[jax.ShapeDtypeStruct((t, dr), F32), jax.ShapeDtypeStruct((1, dr), F32)],
        grid=(t // tm,),
        in_specs=[row, row, _resident(cw.shape), vec, _resident(wg.shape), vec, vec, vec],
        out_specs=[row, pl.BlockSpec((1, dr), lambda i: (0, 0))],
        scratch_shapes=[pltpu.VMEM((tm + SUBLANES, dr), F32), pltpu.VMEM((tm, dr), F32),
                        pltpu.VMEM((tm, dr), F32), pltpu.VMEM((1, dr), F32)],
        compiler_params=_cparams(("arbitrary",)),
        name="rglru_prompt",
    )(xr, gg, cw, cb.reshape(1, dr), wg, ba.reshape(1, dr), bx.reshape(1, dr), lam.reshape(1, dr))


def _rglru_sample_kernel(xcat_ref, gg_ref, h0_ref, cw_ref, cb_ref, wg_ref, ba_ref, bx_ref, lam_ref, ya_ref, h_ref,
                         xc_scr, a_scr, b_scr):
    nseq, _, dr = xcat_ref.shape
    cwid = cw_ref.shape[0]

    def conv(s, _):
        xc = cb_ref[...] + jnp.zeros((SUBLANES, dr), F32)
        for j in range(cwid):
            xc = xc + cw_ref[j:j + 1, :] * xcat_ref[s, j:j + SUBLANES, :]
        xc_scr[pl.ds(pl.multiple_of(s * SUBLANES, SUBLANES), SUBLANES), :] = xc
        return 0

    lax.fori_loop(0, nseq, conv, 0)
    _rglru_gates(xc_scr[...], wg_ref, ba_ref, bx_ref, lam_ref, a_scr, b_scr)

    def body(s, _):
        rows = pl.ds(pl.multiple_of(s * SUBLANES, SUBLANES), SUBLANES)
        a, b = _scan8(a_scr[rows, :], b_scr[rows, :])
        h = a * h0_ref[s] + b
        h_ref[rows, :] = h
        ya_ref[rows, :] = h * gg_ref[rows, :]
        return 0

    lax.fori_loop(0, nseq, body, 0)


def _rglru_sample_call(xcat, gg, h0, cw, cb, wg, ba, bx, lam):
    nseq, _, dr = xcat.shape
    t = nseq * SUBLANES
    full = lambda shape: pl.BlockSpec(shape, lambda i: (0,) * len(shape))
    return pl.pallas_call(
        _rglru_sample_kernel,
        out_shape=[jax.ShapeDtypeStruct((t, dr), F32), jax.ShapeDtypeStruct((t, dr), F32)],
        grid=(1,),
        in_specs=[full(xcat.shape), full((t, dr)), full((nseq, 1, dr)), full(cw.shape), full((1, dr)),
                  full(wg.shape), full((1, dr)), full((1, dr)), full((1, dr))],
        out_specs=[full((t, dr)), full((t, dr))],
        scratch_shapes=[pltpu.VMEM((t, dr), F32), pltpu.VMEM((t, dr), F32), pltpu.VMEM((t, dr), F32)],
        compiler_params=_cparams(("arbitrary",)),
        name="rglru_sample",
    )(xcat, gg, h0.reshape(nseq, 1, dr), cw, cb.reshape(1, dr), wg, ba.reshape(1, dr), bx.reshape(1, dr),
      lam.reshape(1, dr))


def _select_blocks(gs, n_eligible, axis):
    nb = gs.shape[axis]
    jj = lax.broadcasted_iota(jnp.int32, gs.shape, axis)
    s = jnp.where(jj < n_eligible, gs, -jnp.inf)
    sel = jnp.zeros(gs.shape, F32)
    for p in range(MOBA_TOP_K):
        m = jnp.max(s, axis=axis, keepdims=True)
        idx = jnp.min(jnp.where(s == m, jj, nb), axis=axis, keepdims=True)
        hit = jj == idx
        sel = jnp.where(jnp.logical_and(hit, jj * 0 + p < n_eligible), 1.0, sel)
        s = jnp.where(hit, -jnp.inf, s)
    return sel


def _attn_prompt_kernel(q_ref, k_ref, vt_ref, km_ref, near_ref, e_ref, place_ref, o_ref,
                        qa_scr, fm_scr, m_scr, acc_scr, s0_scr, s1_scr):
    i = pl.program_id(1)
    tq, hd2 = q_ref.shape
    hd = hd2 // 2
    blk = MOBA_BLOCK
    cb = FAR_BLOCKS * blk
    nb = km_ref.shape[0]
    nchunk = nb // FAR_BLOCKS
    q2 = q_ref[...]
    qt = jnp.concatenate([q2[:, :hd].T, q2[:, hd:].T], axis=1)
    sel = _select_blocks(_dot_f32(km_ref[...], qt), i, 0)
    jj = lax.broadcasted_iota(jnp.int32, sel.shape, 0)
    adj_sel = jnp.max(jnp.where(jj == i - 1, sel, 0.0), axis=0, keepdims=True)
    far_sel = jnp.where(jj < i - 1, sel, 0.0).astype(BF16)
    placed = _dot(place_ref[...], far_sel)
    prow = lax.broadcasted_iota(jnp.int32, placed.shape, 0)
    fm = jnp.where(prow % BF16_ROWS < FAR_BLOCKS, jnp.where(placed > 0.5, 0.0, NEG), 0.0)
    fm_scr[0:nchunk * BF16_ROWS, :] = fm.astype(BF16)
    srow = lax.broadcasted_iota(jnp.int32, (BF16_ROWS, 2 * tq), 0)
    fm_scr[nchunk * BF16_ROWS:(nchunk + 1) * BF16_ROWS, :] = jnp.where(srow < FAR_BLOCKS, NEG, 0.0).astype(BF16)
    qa_scr[0:hd, :] = (qt * LOG2E).astype(BF16)
    qa_scr[hd + BF16_ROWS:2 * hd, :] = jnp.zeros((hd - BF16_ROWS, 2 * tq), BF16)
    qtb = qa_scr[0:hd, :]

    ja = pl.multiple_of(jnp.maximum(i - 1, 0) * blk, blk)
    jo = pl.multiple_of(i * blk, blk)
    s_adj = _dot(k_ref[pl.ds(ja, blk), :], qtb) + (near_ref[0] + jnp.where(adj_sel > 0.5, 0.0, NEG))
    s_own = _dot(k_ref[pl.ds(jo, blk), :], qtb) + near_ref[1]
    m = jnp.maximum(jnp.max(s_adj, axis=0, keepdims=True), jnp.max(s_own, axis=0, keepdims=True))
    p_adj = jnp.exp2(s_adj - m).astype(BF16)
    p_own = jnp.exp2(s_own - m).astype(BF16)
    acc_scr[...] = _dot(vt_ref[:, pl.ds(ja, blk)], p_adj) + _dot(vt_ref[:, pl.ds(jo, blk)], p_own)
    m_scr[...] = m

    nch = jnp.maximum(i + FAR_BLOCKS - 2, 0) // FAR_BLOCKS

    def chunk_offset(c):
        return pl.multiple_of(jnp.minimum(c, nchunk - 1) * cb, cb)

    def scores(c, s_ref):
        slab = pl.multiple_of(jnp.where(c < nch, c, nchunk) * BF16_ROWS, BF16_ROWS)
        qa_scr[hd:hd + BF16_ROWS, :] = fm_scr[pl.ds(slab, BF16_ROWS), :]
        kc = jnp.concatenate([k_ref[pl.ds(chunk_offset(c), cb), :], e_ref[...]], axis=1)
        s_ref[...] = _dot(kc, qa_scr[...])

    def accumulate(c, s_ref):
        st = s_ref[...]
        m_old = m_scr[...]
        m_new = jnp.maximum(m_old, jnp.max(st, axis=0, keepdims=True))
        p = jnp.exp2(st - m_new).astype(BF16)
        acc_scr[...] = jnp.exp2(m_old - m_new) * acc_scr[...] + _dot(vt_ref[:, pl.ds(chunk_offset(c), cb)], p)
        m_scr[...] = m_new

    scores(0, s0_scr)

    def far_body(t, _):
        scores(2 * t + 1, s1_scr)
        accumulate(2 * t, s0_scr)
        scores(2 * t + 2, s0_scr)
        accumulate(2 * t + 1, s1_scr)
        return 0

    lax.fori_loop(0, (nch + 1) // 2, far_body, 0)

    acc = acc_scr[...]
    ot = acc[:hd] / acc[hd:hd + 1]
    o_ref[:, :hd] = ot[:, :tq].T
    o_ref[:, hd:] = ot[:, tq:].T


def _attn_prompt_call(q, kb, vt, km, near, nkv, hd):
    s = q.shape[0]
    tq = MOBA_BLOCK
    nb = s // tq
    assert nb % FAR_BLOCKS == 0 and nb <= hd
    cb = FAR_BLOCKS * tq
    nchunk = nb // FAR_BLOCKS
    hv = hd + BF16_ROWS
    e = (jnp.arange(hd)[None, :] == (jnp.arange(cb) // tq)[:, None]).astype(BF16)
    pr = jnp.arange(nchunk * BF16_ROWS)
    place = ((pr % BF16_ROWS < FAR_BLOCKS)[:, None]
             & ((pr // BF16_ROWS * FAR_BLOCKS + pr % BF16_ROWS)[:, None] == jnp.arange(nb)[None, :])).astype(BF16)
    return pl.pallas_call(
        _attn_prompt_kernel,
        out_shape=jax.ShapeDtypeStruct(q.shape, F32),
        grid=(nkv, nb),
        in_specs=[
            pl.BlockSpec((tq, 2 * hd), lambda h, i: (i, h)),
            pl.BlockSpec((s, hd), lambda h, i: (0, h)),
            pl.BlockSpec((hv, s), lambda h, i: (h, 0)),
            pl.BlockSpec((nb, hd), lambda h, i: (0, h)),
            pl.BlockSpec((None, 2, tq, 2 * tq), lambda h, i: (h, 0, 0, 0)),
            _resident(e.shape),
            _resident(place.shape),
        ],
        out_specs=pl.BlockSpec((tq, 2 * hd), lambda h, i: (i, h)),
        scratch_shapes=[pltpu.VMEM((2 * hd, 2 * tq), BF16), pltpu.VMEM(((nchunk + 1) * BF16_ROWS, 2 * tq), BF16),
                        pltpu.VMEM((1, 2 * tq), F32), pltpu.VMEM((hv, 2 * tq), F32),
                        pltpu.VMEM((cb, 2 * tq), F32), pltpu.VMEM((cb, 2 * tq), F32)],
        compiler_params=_cparams(("arbitrary", "arbitrary")),
        name="moba_prompt",
    )(q, kb, vt, km, near, e, place)


def _paged(layer, p, npg):
    def index(b, g, pt):
        return (layer, pt[b, g * npg + p], 0, 0)
    return index


def _pv_ones(p, v, ones):
    return _dot(p.astype(BF16), jnp.concatenate([v.astype(BF16), ones[:v.shape[0]]], axis=1))


def _sample_blocks_kernel(pt_ref, q_ref, base_ref, last_ref, *refs, nkv, ppb):
    npg = (len(refs) - 2) // 2
    kpages, vpages = refs[:npg], refs[npg:2 * npg]
    part_ref, km_ref = refs[2 * npg:]
    is_last = pl.program_id(1) == pl.num_programs(1) - 1
    rows, hd = q_ref.shape
    prow = kpages[0].shape[0]
    qb = (q_ref[...] * LOG2E).astype(BF16)
    ones = jnp.ones((prow, hd), BF16)
    lane = lax.broadcasted_iota(jnp.int32, (rows, 2 * hd), 1)
    bps = npg // ppb
    for b in range(bps):
        tot = jnp.zeros((SUBLANES, hd), F32)
        ss = []
        for r in range(ppb):
            kp = kpages[b * ppb + r][...]
            tot = tot + jnp.sum(kp.reshape(prow // SUBLANES, SUBLANES, hd), axis=0)
            tile = base_ref[...]
            if b == bps - 1:
                tile = jnp.where(is_last, last_ref[r], tile)
            ss.append(_dot_nt(qb, kp.astype(BF16)) + tile)
        tot = tot + pltpu.roll(tot, nkv, 0)
        km_ref[b] = tot * (1.0 / (ppb * (prow // nkv)))
        mx = ss[0]
        for s in ss[1:]:
            mx = jnp.maximum(mx, s)
        m = jnp.max(mx, axis=1, keepdims=True)
        part = _pv_ones(jnp.exp2(ss[0] - m), vpages[b * ppb][...], ones)
        for r in range(1, ppb):
            part = part + _pv_ones(jnp.exp2(ss[r] - m), vpages[b * ppb + r][...], ones)
        part_ref[b] = jnp.where(lane == 2 * hd - 1, m, part)


def _sample_blocks_call(qall, base, last, ck, cv, page_table, layer, nb, nkv):
    _, _, prow, hd = ck.shape
    bs, npages = page_table.shape
    ppb = npages // nb
    npg = PAGES_PER_STEP
    assert 2 * nkv == SUBLANES and npg % ppb == 0 and npages % npg == 0
    bps = npg // ppb
    rows = qall.shape[1]
    page_spec = lambda p: pl.BlockSpec((None, None, prow, hd), _paged(layer, p, npg))
    const = lambda shape: pl.BlockSpec(shape, lambda b, g, pt: (0,) * len(shape))
    grid_spec = pltpu.PrefetchScalarGridSpec(
        num_scalar_prefetch=1,
        grid=(bs, npages // npg),
        in_specs=[pl.BlockSpec((None, rows, hd), lambda b, g, pt: (b, 0, 0)), const(base.shape), const(last.shape)]
        + [page_spec(p) for p in range(npg)] * 2,
        out_specs=[pl.BlockSpec((None, bps, rows, 2 * hd), lambda b, g, pt: (b, g, 0, 0)),
                   pl.BlockSpec((None, bps, SUBLANES, hd), lambda b, g, pt: (b, g, 0, 0))],
    )
    return pl.pallas_call(
        functools.partial(_sample_blocks_kernel, nkv=nkv, ppb=ppb),
        out_shape=[jax.ShapeDtypeStruct((bs, nb, rows, 2 * hd), F32),
                   jax.ShapeDtypeStruct((bs, nb, SUBLANES, hd), F32)],
        grid_spec=grid_spec,
        compiler_params=_cparams(("parallel", "parallel")),
        name="sample_blocks",
    )(page_table, qall, base, last, *([ck] * npg), *([cv] * npg))


def _sample_merge_kernel(q_ref, kn_ref, vn_ref, km_ref, own_ref, part_ref, o_ref, *, nkv):
    rows, hd = q_ref.shape
    rph = rows // nkv
    nb = km_ref.shape[1]
    q = q_ref[...]
    qb = (q * LOG2E).astype(BF16)
    sel = jnp.concatenate(
        [_select_blocks(_dot_nt_f32(q[h * rph:(h + 1) * rph], km_ref[h]), nb, 1) for h in range(nkv)], axis=0)
    pad = jnp.zeros((LANES - kn_ref.shape[0], hd), F32)
    s = _dot_nt(qb, jnp.concatenate([kn_ref[...], pad], axis=0).astype(BF16)) + own_ref[...]
    m0 = jnp.max(s, axis=1, keepdims=True)
    acc0 = _pv_ones(jnp.exp2(s - m0), jnp.concatenate([vn_ref[...], pad], axis=0), jnp.ones((LANES, hd), BF16))
    lane = lax.broadcasted_iota(jnp.int32, (rows, nb), 1)

    def body(j, carry):
        m, acc = carry
        part = part_ref[j]
        picked = jnp.sum(jnp.where(lane == j, sel, 0.0), axis=1, keepdims=True) > 0.5
        mj = part[:, 2 * hd - 1:2 * hd]
        m_new = jnp.maximum(m, jnp.where(picked, mj, NEG))
        w = jnp.where(picked, jnp.exp2(mj - m_new), 0.0)
        return m_new, jnp.exp2(m - m_new) * acc + w * part

    _, acc = lax.fori_loop(0, nb, body, (m0, acc0))
    o_ref[...] = acc[:, :hd] / acc[:, hd:hd + 1]


def _sample_merge_call(qall, kn, vn, km, own, parts, nkv):
    bs, rows, hd = qall.shape
    nb = km.shape[2]
    nown = kn.shape[1]
    assert nown <= LANES
    per_seq = lambda *shape: pl.BlockSpec((None,) + shape, lambda b: (b,) + (0,) * len(shape))
    return pl.pallas_call(
        functools.partial(_sample_merge_kernel, nkv=nkv),
        out_shape=jax.ShapeDtypeStruct(qall.shape, F32),
        grid=(bs,),
        in_specs=[per_seq(rows, hd), per_seq(nown, hd), per_seq(nown, hd), per_seq(nkv, nb, hd),
                  pl.BlockSpec(own.shape, lambda b: (0, 0)), per_seq(nb, rows, 2 * hd)],
        out_specs=per_seq(rows, hd),
        compiler_params=_cparams(("parallel",)),
        name="sample_merge",
    )(qall, kn, vn, km, own, parts)


def _rel_bucket(dist, n_buckets):
    n = jnp.maximum(dist, 0)
    max_exact = n_buckets // 2
    nf = jnp.maximum(n, 1).astype(F32)
    large = max_exact + (jnp.log(nf / max_exact) / math.log(MAX_DISTANCE / max_exact)
                         * (n_buckets - max_exact)).astype(jnp.int32)
    large = jnp.minimum(large, n_buckets - 1)
    return jnp.where(n < max_exact, n, large)


def _shifted_bias(dist, rel_table):
    nbk = rel_table.shape[0]
    far = rel_table[_rel_bucket(jnp.asarray(2 * MOBA_BLOCK), nbk)]
    b = (rel_table[_rel_bucket(dist, nbk)] - far) * LOG2E
    return jnp.where((dist >= 0)[..., None], b, NEG)


def _skew_toeplitz(u, n):
    lead = u.shape[:-1]
    up = jnp.concatenate([u, jnp.zeros(lead + (1,), u.dtype)], axis=-1)
    flat = jnp.broadcast_to(up[..., None, :], lead + (n, 2 * n)).reshape(lead + (2 * n * n,))
    return flat[..., :n * (2 * n - 1)].reshape(lead + (n, 2 * n - 1))


def _prompt_near_tiles(rel_table, nkv):
    blk = MOBA_BLOCK
    n = 2 * blk
    u = _shifted_bias(jnp.arange(2 * n - 1) - (blk - 1), rel_table).T
    a = _skew_toeplitz(u, n)[:, :, n - 1:n - 1 + blk]
    a = a.reshape(nkv, 2, 2, blk, blk)
    return jnp.transpose(a, (0, 2, 3, 1, 4)).reshape(nkv, 2, blk, 2 * blk)


def _sample_tiles(rel_table, nkv, lq, psz, ppb):
    rows = 2 * nkv * lq
    row_kvh = jnp.arange(rows) // (2 * lq)
    row_t = jnp.arange(rows) % lq
    col_h = jnp.arange(psz * nkv) % nkv
    base = jnp.where(row_kvh[:, None] == col_h[None, :], 0.0, NEG).astype(F32)
    kk = jnp.arange(ppb * psz)
    b = _shifted_bias(jnp.arange(lq)[:, None] + ppb * psz - kk[None, :], rel_table)
    b = jnp.transpose(b, (2, 0, 1)).reshape(rows, ppb, psz)
    last = jnp.transpose(b, (1, 0, 2))[:, :, :, None] + base.reshape(1, rows, psz, nkv)
    last = last.reshape(ppb, rows, psz * nkv)
    ncol = LANES
    col_t, col_hh = jnp.arange(ncol) // nkv, jnp.arange(ncol) % nkv
    ob = _shifted_bias(jnp.arange(lq)[:, None] - col_t[None, :], rel_table)
    ob = jnp.transpose(ob, (2, 0, 1)).reshape(rows, ncol)
    ok = (row_kvh[:, None] == col_hh[None, :]) & (col_t[None, :] < lq)
    own = jnp.where(ok, ob, NEG)
    return base, last, own


def _block_diag_gates(w_a, w_x, group):
    n, c, _ = w_a.shape
    eye = jnp.eye(group, dtype=w_a.dtype)

    def bd(w):
        w = w.reshape(n // group, group, c, c)
        return jnp.einsum('gbcd,be->gbced', w, eye).reshape(n // group, group * c, group * c)

    return jnp.concatenate([bd(w_a), bd(w_x)], axis=-1).astype(BF16)


def kernel(x_prompt, x_sample, c_prompt, c_sample, cache_k, cache_v, page_table, state_conv, state_rglru, norm_g,
           w_ada, b_ada, w_in, conv_w, conv_b, w_rg_a, b_rg_a, w_rg_x, b_rg_x, lru_lambda, q_norm_g, k_norm_g,
           rel_table, w_out, w_ffn_in, w_ffn_out):
    depth = norm_g.shape[0]
    bp, s, d = x_prompt.shape
    bs, lq, _ = x_sample.shape
    _, npool, psz, nkv, hd = cache_k.shape
    nh = rel_table.shape[1]
    dr = conv_w.shape[2]
    cwid = conv_w.shape[1]
    past_len = page_table.shape[1] * psz
    nb_past = past_len // MOBA_BLOCK
    assert bp == 1 and s % MOBA_BLOCK == 0 and lq == SUBLANES and nh == 2 * nkv
    assert past_len == nb_past * MOBA_BLOCK and MOBA_BLOCK % psz == 0 and cwid - 1 <= lq
    assert MAX_DISTANCE <= MOBA_BLOCK
    dkv = nkv * hd
    ppb = MOBA_BLOCK // psz
    tm_p = 512 if s % 512 == 0 else MOBA_BLOCK
    ts = bs * lq

    c_all = jnp.concatenate([c_prompt, c_sample], axis=0)
    nrow = -(-c_all.shape[0] // SUBLANES) * SUBLANES
    c_all = jnp.pad(c_all, ((0, nrow - c_all.shape[0]), (0, 0)))
    mods = _ada_call(c_all, w_ada, b_ada)

    ck = cache_k.reshape(depth, npool, psz * nkv, hd)
    cv = cache_v.reshape(depth, npool, psz * nkv, hd)
    near_p = _prompt_near_tiles(rel_table, nkv)
    base_s, last_s, own_s = _sample_tiles(rel_table, nkv, lq, psz, ppb)

    xp = x_prompt.reshape(s, d)
    xs = x_sample.reshape(ts, d)
    outs = {k: [] for k in ('kp', 'vp', 'cp', 'hp', 'ks', 'vs', 'cs', 'hs')}
    for l in range(depth):
        mod_p = mods[l, 0:bp]
        mod_s = jnp.repeat(mods[l, bp:bp + bs], lq, axis=0)
        wup0, wup1 = w_ffn_in[l, 0].astype(BF16), w_ffn_in[l, 1].astype(BF16)
        wdn0, wdn1 = w_ffn_out[l, 0].astype(BF16), w_ffn_out[l, 1].astype(BF16)
        win = w_in[l].astype(BF16)
        wout = w_out[l].astype(BF16)
        wg = _block_diag_gates(w_rg_a[l], w_rg_x[l], 4)
        rg = (conv_w[l], conv_b[l], wg, b_rg_a[l], b_rg_x[l], lru_lambda[l])
        proj = dict(dr=dr, nh=nh, nkv=nkv, hd=hd)

        xp = _ffn_call(xp, mod_p, 0, norm_g[l, 0], wup0, wdn0, tm_p)
        xr, gg, sgb, q, k, v, kb, vt, km = _inproj_call(xp, mod_p, norm_g[l, 1], win, q_norm_g[l], k_norm_g[l], tm_p,
                                                        prompt=True, **proj)
        ya, hlast = _rglru_prompt_call(xr, gg, *rg, tm_p)
        yatt = _attn_prompt_call(q, kb, vt, km.reshape(s // MOBA_BLOCK, dkv), near_p, nkv, hd)
        xp = _outproj_ffn_call(xp, ya, sgb, yatt, wout, mod_p, 1, 2, norm_g[l, 2], wup1, wdn1, tm_p)
        outs['kp'].append(k.reshape(bp, s, nkv, hd))
        outs['vp'].append(v.reshape(bp, s, nkv, hd))
        outs['cp'].append(xr[s - (cwid - 1):].reshape(bp, cwid - 1, dr))
        outs['hp'].append(hlast)

        xs = _ffn_call(xs, mod_s, 0, norm_g[l, 0], wup0, wdn0, ts)
        xr, gg, sgb, q, k, v = _inproj_call(xs, mod_s, norm_g[l, 1], win, q_norm_g[l], k_norm_g[l], ts,
                                            prompt=False, **proj)
        xr3 = xr.reshape(bs, lq, dr)
        xcat = jnp.concatenate([state_conv[l], xr3, jnp.zeros((bs, 2 * SUBLANES - lq - (cwid - 1), dr), F32)], axis=1)
        ya, h_all = _rglru_sample_call(xcat, gg, state_rglru[l], *rg)
        qall = jnp.transpose(q.reshape(bs, lq, nkv, 2, hd), (0, 2, 3, 1, 4)).reshape(bs, 2 * nkv * lq, hd)
        parts, km_s = _sample_blocks_call(qall, base_s, last_s, ck, cv, page_table, l, nb_past, nkv)
        km_s = jnp.transpose(km_s[:, :, :nkv], (0, 2, 1, 3))
        yatt = _sample_merge_call(qall, k.reshape(bs, lq * nkv, hd), v.reshape(bs, lq * nkv, hd), km_s,
                                  own_s, parts, nkv)
        yatt = jnp.transpose(yatt.reshape(bs, nkv, 2, lq, hd), (0, 3, 1, 2, 4)).reshape(ts, nh * hd)
        xs = _outproj_ffn_call(xs, ya, sgb, yatt, wout, mod_s, 1, 2, norm_g[l, 2], wup1, wdn1, ts)
        outs['ks'].append(k.reshape(bs, lq, nkv, hd))
        outs['vs'].append(v.reshape(bs, lq, nkv, hd))
        outs['cs'].append(xr3[:, lq - (cwid - 1):])
        outs['hs'].append(h_all.reshape(bs, lq, dr)[:, lq - 1])

    st = {k: jnp.stack(v) for k, v in outs.items()}
    return (xp.reshape(bp, s, d), xs.reshape(bs, lq, d), st['kp'], st['vp'], st['cp'], st['hp'],
            st['ks'], st['vs'], st['cs'], st['hs'])
```

```python
import functools
import math

import jax
import jax.numpy as jnp
from jax import lax
from jax.experimental import pallas as pl
from jax.experimental.pallas import tpu as pltpu

F32 = jnp.float32
BF16 = jnp.bfloat16

EPS = 1e-6
LRU_C = 8.0
MOBA_BLOCK = 256
MOBA_TOP_K = 3
MAX_DISTANCE = 128
NEG = -1e30
LOG2E = 1.4426950408889634

V7X_VMEM_BYTES = 64 * 1024 * 1024
VMEM_LIMIT = V7X_VMEM_BYTES * 3 // 4
SUBLANES = 8
LANES = 128
BF16_ROWS = 16
PAGES_PER_STEP = 16
FAR_BLOCKS = 4


def _cparams(sem):
    return pltpu.CompilerParams(dimension_semantics=sem, vmem_limit_bytes=VMEM_LIMIT)


def _resident(shape):
    nd = len(shape)
    return pl.BlockSpec(shape, lambda *_: (0,) * nd, pipeline_mode=pl.Buffered(1))


def _sigmoid(x):
    return 1.0 / (1.0 + jnp.exp(-x))


def _gelu_tanh(x):
    return 0.5 * x * (1.0 + jnp.tanh(math.sqrt(2.0 / math.pi) * (x + 0.044715 * (x * x * x))))


def _rms_mod(x, g, shift, scale):
    ms = jnp.mean(x * x, axis=-1, keepdims=True)
    y = x * lax.rsqrt(ms + EPS) * g
    return y * (1.0 + scale) + shift


def _dot(a, b):
    return jnp.dot(a, b, preferred_element_type=F32)


def _dot_nt(a, b):
    return lax.dot_general(a, b, (((1,), (1,)), ((), ())), preferred_element_type=F32)


def _dot_f32(a, b):
    return jnp.dot(a, b, preferred_element_type=F32, precision=lax.Precision.HIGHEST)


def _dot_nt_f32(a, b):
    return lax.dot_general(a, b, (((1,), (1,)), ((), ())), preferred_element_type=F32,
                           precision=lax.Precision.HIGHEST)


def _ada_kernel(c_ref, w_ref, b_ref, o_ref):
    c = c_ref[...]
    o_ref[...] = _dot_f32(c * _sigmoid(c), w_ref[...]) + b_ref[...]


def _ada_call(c_all, w_ada, b_ada):
    depth, d, n = w_ada.shape
    rows = c_all.shape[0]
    tn = 1024
    return pl.pallas_call(
        _ada_kernel,
        out_shape=jax.ShapeDtypeStruct((depth, rows, n), F32),
        grid=(depth, n // tn),
        in_specs=[
            pl.BlockSpec((rows, d), lambda l, j: (0, 0)),
            pl.BlockSpec((None, d, tn), lambda l, j: (l, 0, j)),
            pl.BlockSpec((None, 1, tn), lambda l, j: (l, 0, j)),
        ],
        out_specs=pl.BlockSpec((None, rows, tn), lambda l, j: (l, 0, j)),
        compiler_params=_cparams(("arbitrary", "arbitrary")),
        name="adaln",
    )(c_all, w_ada, b_ada.reshape(depth, 1, n))


def _mod_spec(mod, tm, sub, d):
    if mod.shape[0] == 1:
        return pl.BlockSpec((1, 3 * d), lambda i: (0, sub))
    return pl.BlockSpec((tm, 3 * d), lambda i: (i, sub))


def _ffn_body(x, mod_ref, g_ref, wup_ref, wdn_ref, fc):
    d = x.shape[-1]
    f = wdn_ref.shape[0]
    h = _rms_mod(x, g_ref[...], mod_ref[:, 0:d], mod_ref[:, d:2 * d]).astype(BF16)
    acc = jnp.zeros(x.shape, F32)
    for c in range(f // fc):
        a = _dot(h, wup_ref[:, c * fc:(c + 1) * fc])
        g = _dot(h, wup_ref[:, f + c * fc:f + (c + 1) * fc])
        act = (g * _sigmoid(g) * a).astype(BF16)
        acc = acc + _dot(act, wdn_ref[c * fc:(c + 1) * fc, :])
    return x + 0.5 * mod_ref[:, 2 * d:3 * d] * acc


def _ffn_kernel(x_ref, mod_ref, g_ref, wup_ref, wdn_ref, o_ref, *, fc):
    o_ref[...] = _ffn_body(x_ref[...], mod_ref, g_ref, wup_ref, wdn_ref, fc)


def _outproj_ffn_kernel(x_ref, ya_ref, sgb_ref, yatt_ref, wout_ref, modo_ref, mod_ref, g_ref, wup_ref, wdn_ref,
                        o_ref, *, fc):
    x = x_ref[...]
    d = x.shape[-1]
    merged = (ya_ref[...] + sgb_ref[...] * yatt_ref[...]).astype(BF16)
    x = x + modo_ref[:, 2 * d:3 * d] * _dot(merged, wout_ref[...])
    o_ref[...] = _ffn_body(x, mod_ref, g_ref, wup_ref, wdn_ref, fc)


def _ffn_chunk(f):
    for fc in (512, 256, 128):
        if f % fc == 0:
            return fc
    return f


def _ffn_call(x, mod, sub, g, wup, wdn, tm):
    t, d = x.shape
    row = pl.BlockSpec((tm, d), lambda i: (i, 0))
    return pl.pallas_call(
        functools.partial(_ffn_kernel, fc=_ffn_chunk(wdn.shape[0])),
        out_shape=jax.ShapeDtypeStruct((t, d), F32),
        grid=(t // tm,),
        in_specs=[row, _mod_spec(mod, tm, sub, d), _resident((1, d)), _resident(wup.shape), _resident(wdn.shape)],
        out_specs=row,
        compiler_params=_cparams(("parallel",)),
        name="ffn",
    )(x, mod, g.reshape(1, d), wup, wdn)


def _outproj_ffn_call(x, ya, sgb, yatt, wout, mod, sub_out, sub, g, wup, wdn, tm):
    t, d = x.shape
    row = pl.BlockSpec((tm, d), lambda i: (i, 0))
    return pl.pallas_call(
        functools.partial(_outproj_ffn_kernel, fc=_ffn_chunk(wdn.shape[0])),
        out_shape=jax.ShapeDtypeStruct((t, d), F32),
        grid=(t // tm,),
        in_specs=[row, row, row, row, _resident(wout.shape), _mod_spec(mod, tm, sub_out, d),
                  _mod_spec(mod, tm, sub, d), _resident((1, d)), _resident(wup.shape), _resident(wdn.shape)],
        out_specs=row,
        compiler_params=_cparams(("parallel",)),
        name="outproj_ffn",
    )(x, ya, sgb, yatt, wout, mod, mod, g.reshape(1, d), wup, wdn)


def _head_rms(u, g, nheads, hd, scale):
    outs = []
    for h in range(nheads):
        uh = u[:, h * hd:(h + 1) * hd]
        ms = jnp.mean(uh * uh, axis=-1, keepdims=True)
        outs.append(uh * lax.rsqrt(ms + EPS) * g * scale)
    return outs


def _inproj_kernel(x_ref, mod_ref, g_ref, win_ref, qg_ref, kg_ref, *out_refs, dr, nh, nkv, hd, prompt):
    if prompt:
        xr_ref, gg_ref, sgb_ref, q_ref, k_ref, v_ref, kb_ref, vt_ref, km_ref = out_refs
    else:
        xr_ref, gg_ref, sgb_ref, q_ref, k_ref, v_ref = out_refs
    x = x_ref[...]
    tm, d = x.shape
    h = _rms_mod(x, g_ref[...], mod_ref[:, 0:d], mod_ref[:, d:2 * d]).astype(BF16)
    dq, dkv = nh * hd, nkv * hd
    o_q = 2 * dr
    o_k = o_q + dq
    o_v = o_k + dkv
    o_ga = o_v + dkv
    o_gb = o_ga + d

    def seg(lo, width):
        return _dot(h, win_ref[:, lo:lo + width])

    xr_ref[...] = seg(0, dr)
    gg_ref[...] = _sigmoid(seg(o_ga, d)) * _gelu_tanh(seg(dr, dr))
    sgb_ref[...] = _sigmoid(seg(o_gb, d))
    qs = _head_rms(seg(o_q, dq), qg_ref[...], nh, hd, hd ** -0.5)
    for hh in range(nh):
        q_ref[:, hh * hd:(hh + 1) * hd] = qs[hh]
    ks = _head_rms(seg(o_k, dkv), kg_ref[...], nkv, hd, 1.0)
    v = seg(o_v, dkv)
    for hh in range(nkv):
        k_ref[pl.ds(hh, tm, stride=nkv), :] = ks[hh]
        v_ref[pl.ds(hh, tm, stride=nkv), :] = v[:, hh * hd:(hh + 1) * hd]
    if prompt:
        vt = v.T.astype(BF16)
        hv = hd + BF16_ROWS
        for hh in range(nkv):
            vt_ref[hh * hv:hh * hv + hd, :] = vt[hh * hd:(hh + 1) * hd, :]
            vt_ref[hh * hv + hd:(hh + 1) * hv, :] = jnp.ones((BF16_ROWS, tm), BF16)
            kb_ref[:, hh * hd:(hh + 1) * hd] = ks[hh].astype(BF16)
            for b in range(tm // MOBA_BLOCK):
                km_ref[b, :, hh * hd:(hh + 1) * hd] = jnp.mean(
                    ks[hh][b * MOBA_BLOCK:(b + 1) * MOBA_BLOCK], axis=0, keepdims=True)


def _inproj_call(x, mod, g, win, qg, kg, tm, *, dr, nh, nkv, hd, prompt):
    t, d = x.shape
    dq, dkv = nh * hd, nkv * hd
    row = lambda w: pl.BlockSpec((tm, w), lambda i: (i, 0))
    out_shape = [jax.ShapeDtypeStruct((t, w), F32) for w in (dr, d, d, dq)]
    out_specs = [row(w) for w in (dr, d, d, dq)]
    out_shape += [jax.ShapeDtypeStruct((t * nkv, hd), F32)] * 2
    out_specs += [pl.BlockSpec((tm * nkv, hd), lambda i: (i, 0))] * 2
    if prompt:
        nbt = tm // MOBA_BLOCK
        dvt = nkv * (hd + BF16_ROWS)
        out_shape += [jax.ShapeDtypeStruct((t, dkv), BF16), jax.ShapeDtypeStruct((dvt, t), BF16),
                      jax.ShapeDtypeStruct((t // MOBA_BLOCK, 1, dkv), F32)]
        out_specs += [row(dkv), pl.BlockSpec((dvt, tm), lambda i: (0, i)),
                      pl.BlockSpec((nbt, 1, dkv), lambda i: (i, 0, 0))]
    return pl.pallas_call(
        functools.partial(_inproj_kernel, dr=dr, nh=nh, nkv=nkv, hd=hd, prompt=prompt),
        out_shape=out_shape,
        grid=(t // tm,),
        in_specs=[row(d), _mod_spec(mod, tm, 1, d), _resident((1, d)), _resident(win.shape),
                  _resident((1, hd)), _resident((1, hd))],
        out_specs=out_specs,
        compiler_params=_cparams(("parallel",)),
        name="inproj",
    )(x, mod, g.reshape(1, d), win, qg.reshape(1, hd), kg.reshape(1, hd))


def _rglru_gates(xc, wg_ref, ba_ref, bx_ref, lam_ref, a_scr, b_scr):
    ngroups, gw, _ = wg_ref.shape
    xcb = xc.astype(BF16)
    for gi in range(ngroups):
        cols = slice(gi * gw, (gi + 1) * gw)
        u = _dot(xcb[:, cols], wg_ref[gi])
        r = _sigmoid(u[:, :gw] + ba_ref[:, cols])
        gate_x = _sigmoid(u[:, gw:] + bx_ref[:, cols])
        lam = lam_ref[:, cols]
        softplus_neg = jnp.maximum(-lam, 0.0) + jnp.log(1.0 + jnp.exp(-jnp.abs(lam)))
        log_a = -LRU_C * r * softplus_neg
        a = jnp.exp(log_a)
        mult = jnp.sqrt(-_expm1(2.0 * log_a))
        a_scr[:, cols] = a
        b_scr[:, cols] = mult * gate_x * xc[:, cols]


def _expm1(x):
    small = x * (1.0 + x * (0.5 + x * (1.0 / 6.0 + x * (1.0 / 24.0 + x * (1.0 / 120.0)))))
    return jnp.where(x > -0.1, small, jnp.exp(x) - 1.0)


def _scan8(a, b):
    row = lax.broadcasted_iota(jnp.int32, a.shape, 0)
    for k in (1, 2, 4):
        keep = row >= k
        a_sh = jnp.where(keep, pltpu.roll(a, k, 0), 1.0)
        b_sh = jnp.where(keep, pltpu.roll(b, k, 0), 0.0)
        b = a * b_sh + b
        a = a * a_sh
    return a, b


def _rglru_prompt_kernel(xr_ref, gg_ref, cw_ref, cb_ref, wg_ref, ba_ref, bx_ref, lam_ref, ya_ref, hlast_ref,
                         xbuf, a_scr, b_scr, hcar):
    i = pl.program_id(0)
    tm, dr = xr_ref.shape
    cwid = cw_ref.shape[0]

    @pl.when(i == 0)
    def _():
        xbuf[0:SUBLANES, :] = jnp.zeros((SUBLANES, dr), F32)
        hcar[...] = jnp.zeros_like(hcar)

    xbuf[SUBLANES:SUBLANES + tm, :] = xr_ref[...]
    xc = cb_ref[...] + jnp.zeros((tm, dr), F32)
    for j in range(cwid):
        off = SUBLANES - (cwid - 1) + j
        xc = xc + cw_ref[j:j + 1, :] * xbuf[off:off + tm, :]
    xbuf[0:SUBLANES, :] = xbuf[tm:tm + SUBLANES, :]
    _rglru_gates(xc, wg_ref, ba_ref, bx_ref, lam_ref, a_scr, b_scr)

    def body(g, carry):
        rows = pl.ds(pl.multiple_of(g * SUBLANES, SUBLANES), SUBLANES)
        a, b = _scan8(a_scr[rows, :], b_scr[rows, :])
        h = a * carry + b
        ya_ref[rows, :] = h * gg_ref[rows, :]
        return h[SUBLANES - 1:SUBLANES, :]

    hc = lax.fori_loop(0, tm // SUBLANES, body, hcar[...])
    hcar[...] = hc
    hlast_ref[...] = hc


def _rglru_prompt_call(xr, gg, cw, cb, wg, ba, bx, lam, tm):
    t, dr = xr.shape
    row = pl.BlockSpec((tm, dr), lambda i: (i, 0))
    vec = _resident((1, dr))
    return pl.pallas_call(
        _rglru_prompt_kernel,
        out_shape=[jax.ShapeDtypeStruct((t, dr), F32), jax.ShapeDtypeStruct((1, dr), F32)],
        grid=(t // tm,),
        in_specs=[row, row, _resident(cw.shape), vec, _resident(wg.shape), vec, vec, vec],
        out_specs=[row, pl.BlockSpec((1, dr), lambda i: (0, 0))],
        scratch_shapes=[pltpu.VMEM((tm + SUBLANES, dr), F32), pltpu.VMEM((tm, dr), F32),
                        pltpu.VMEM((tm, dr), F32), pltpu.VMEM((1, dr), F32)],
        compiler_params=_cparams(("arbitrary",)),
        name="rglru_prompt",
    )(xr, gg, cw, cb.reshape(1, dr), wg, ba.reshape(1, dr), bx.reshape(1, dr), lam.reshape(1, dr))


def _rglru_sample_kernel(xcat_ref, gg_ref, h0_ref, cw_ref, cb_ref, wg_ref, ba_ref, bx_ref, lam_ref, ya_ref, h_ref,
                         xc_scr, a_scr, b_scr):
    nseq, _, dr = xcat_ref.shape
    cwid = cw_ref.shape[0]

    def conv(s, _):
        xc = cb_ref[...] + jnp.zeros((SUBLANES, dr), F32)
        for j in range(cwid):
            xc = xc + cw_ref[j:j + 1, :] * xcat_ref[s, j:j + SUBLANES, :]
        xc_scr[pl.ds(pl.multiple_of(s * SUBLANES, SUBLANES), SUBLANES), :] = xc
        return 0

    lax.fori_loop(0, nseq, conv, 0)
    _rglru_gates(xc_scr[...], wg_ref, ba_ref, bx_ref, lam_ref, a_scr, b_scr)

    def body(s, _):
        rows = pl.ds(pl.multiple_of(s * SUBLANES, SUBLANES), SUBLANES)
        a, b = _scan8(a_scr[rows, :], b_scr[rows, :])
        h = a * h0_ref[s] + b
        h_ref[rows, :] = h
        ya_ref[rows, :] = h * gg_ref[rows, :]
        return 0

    lax.fori_loop(0, nseq, body, 0)


def _rglru_sample_call(xcat, gg, h0, cw, cb, wg, ba, bx, lam):
    nseq, _, dr = xcat.shape
    t = nseq * SUBLANES
    full = lambda shape: pl.BlockSpec(shape, lambda i: (0,) * len(shape))
    return pl.pallas_call(
        _rglru_sample_kernel,
        out_shape=[jax.ShapeDtypeStruct((t, dr), F32), jax.ShapeDtypeStruct((t, dr), F32)],
        grid=(1,),
        in_specs=[full(xcat.shape), full((t, dr)), full((nseq, 1, dr)), full(cw.shape), full((1, dr)),
                  full(wg.shape), full((1, dr)), full((1, dr)), full((1, dr))],
        out_specs=[full((t, dr)), full((t, dr))],
        scratch_shapes=[pltpu.VMEM((t, dr), F32), pltpu.VMEM((t, dr), F32), pltpu.VMEM((t, dr), F32)],
        compiler_params=_cparams(("arbitrary",)),
        name="rglru_sample",
    )(xcat, gg, h0.reshape(nseq, 1, dr), cw, cb.reshape(1, dr), wg, ba.reshape(1, dr), bx.reshape(1, dr),
      lam.reshape(1, dr))


def _select_blocks(gs, n_eligible, axis):
    nb = gs.shape[axis]
    jj = lax.broadcasted_iota(jnp.int32, gs.shape, axis)
    s = jnp.where(jj < n_eligible, gs, -jnp.inf)
    sel = jnp.zeros(gs.shape, F32)
    for p in range(MOBA_TOP_K):
        m = jnp.max(s, axis=axis, keepdims=True)
        idx = jnp.min(jnp.where(s == m, jj, nb), axis=axis, keepdims=True)
        hit = jj == idx
        sel = jnp.where(jnp.logical_and(hit, jj * 0 + p < n_eligible), 1.0, sel)
        s = jnp.where(hit, -jnp.inf, s)
    return sel


def _attn_prompt_kernel(q_ref, k_ref, vt_ref, km_ref, near_ref, e_ref, place_ref, o_ref,
                        qa_scr, fm_scr, m_scr, acc_scr, s0_scr, s1_scr):
    i = pl.program_id(1)
    tq, hd2 = q_ref.shape
    hd = hd2 // 2
    blk = MOBA_BLOCK
    cb = FAR_BLOCKS * blk
    nb = km_ref.shape[0]
    nchunk = nb // FAR_BLOCKS
    q2 = q_ref[...]
    qt = jnp.concatenate([q2[:, :hd].T, q2[:, hd:].T], axis=1)
    sel = _select_blocks(_dot_f32(km_ref[...], qt), i, 0)
    jj = lax.broadcasted_iota(jnp.int32, sel.shape, 0)
    adj_sel = jnp.max(jnp.where(jj == i - 1, sel, 0.0), axis=0, keepdims=True)
    far_sel = jnp.where(jj < i - 1, sel, 0.0).astype(BF16)
    placed = _dot(place_ref[...], far_sel)
    prow = lax.broadcasted_iota(jnp.int32, placed.shape, 0)
    fm = jnp.where(prow % BF16_ROWS < FAR_BLOCKS, jnp.where(placed > 0.5, 0.0, NEG), 0.0)
    fm_scr[0:nchunk * BF16_ROWS, :] = fm.astype(BF16)
    srow = lax.broadcasted_iota(jnp.int32, (BF16_ROWS, 2 * tq), 0)
    fm_scr[nchunk * BF16_ROWS:(nchunk + 1) * BF16_ROWS, :] = jnp.where(srow < FAR_BLOCKS, NEG, 0.0).astype(BF16)
    qa_scr[0:hd, :] = (qt * LOG2E).astype(BF16)
    qa_scr[hd + BF16_ROWS:2 * hd, :] = jnp.zeros((hd - BF16_ROWS, 2 * tq), BF16)
    qtb = qa_scr[0:hd, :]

    ja = pl.multiple_of(jnp.maximum(i - 1, 0) * blk, blk)
    jo = pl.multiple_of(i * blk, blk)
    s_adj = _dot(k_ref[pl.ds(ja, blk), :], qtb) + (near_ref[0] + jnp.where(adj_sel > 0.5, 0.0, NEG))
    s_own = _dot(k_ref[pl.ds(jo, blk), :], qtb) + near_ref[1]
    m = jnp.maximum(jnp.max(s_adj, axis=0, keepdims=True), jnp.max(s_own, axis=0, keepdims=True))
    p_adj = jnp.exp2(s_adj - m).astype(BF16)
    p_own = jnp.exp2(s_own - m).astype(BF16)
    acc_scr[...] = _dot(vt_ref[:, pl.ds(ja, blk)], p_adj) + _dot(vt_ref[:, pl.ds(jo, blk)], p_own)
    m_scr[...] = m

    nch = jnp.maximum(i + FAR_BLOCKS - 2, 0) // FAR_BLOCKS

    def chunk_offset(c):
        return pl.multiple_of(jnp.minimum(c, nchunk - 1) * cb, cb)

    def scores(c, s_ref):
        slab = pl.multiple_of(jnp.where(c < nch, c, nchunk) * BF16_ROWS, BF16_ROWS)
        qa_scr[hd:hd + BF16_ROWS, :] = fm_scr[pl.ds(slab, BF16_ROWS), :]
        kc = jnp.concatenate([k_ref[pl.ds(chunk_offset(c), cb), :], e_ref[...]], axis=1)
        s_ref[...] = _dot(kc, qa_scr[...])

    def accumulate(c, s_ref):
        st = s_ref[...]
        m_old = m_scr[...]
        m_new = jnp.maximum(m_old, jnp.max(st, axis=0, keepdims=True))
        p = jnp.exp2(st - m_new).astype(BF16)
        acc_scr[...] = jnp.exp2(m_old - m_new) * acc_scr[...] + _dot(vt_ref[:, pl.ds(chunk_offset(c), cb)], p)
        m_scr[...] = m_new

    scores(0, s0_scr)

    def far_body(t, _):
        scores(2 * t + 1, s1_scr)
        accumulate(2 * t, s0_scr)
        scores(2 * t + 2, s0_scr)
        accumulate(2 * t + 1, s1_scr)
        return 0

    lax.fori_loop(0, nch // 2, far_body, 0)

    @pl.when(nch % 2 == 1)
    def _():
        accumulate(nch - 1, s0_scr)

    acc = acc_scr[...]
    ot = acc[:hd] / acc[hd:hd + 1]
    o_ref[:, :hd] = ot[:, :tq].T
    o_ref[:, hd:] = ot[:, tq:].T


def _attn_prompt_call(q, kb, vt, km, near, nkv, hd):
    s = q.shape[0]
    tq = MOBA_BLOCK
    nb = s // tq
    assert nb % FAR_BLOCKS == 0 and nb <= hd
    cb = FAR_BLOCKS * tq
    nchunk = nb // FAR_BLOCKS
    hv = hd + BF16_ROWS
    e = (jnp.arange(hd)[None, :] == (jnp.arange(cb) // tq)[:, None]).astype(BF16)
    pr = jnp.arange(nchunk * BF16_ROWS)
    place = ((pr % BF16_ROWS < FAR_BLOCKS)[:, None]
             & ((pr // BF16_ROWS * FAR_BLOCKS + pr % BF16_ROWS)[:, None] == jnp.arange(nb)[None, :])).astype(BF16)
    return pl.pallas_call(
        _attn_prompt_kernel,
        out_shape=jax.ShapeDtypeStruct(q.shape, F32),
        grid=(nkv, nb),
        in_specs=[
            pl.BlockSpec((tq, 2 * hd), lambda h, i: (i, h)),
            pl.BlockSpec((s, hd), lambda h, i: (0, h)),
            pl.BlockSpec((hv, s), lambda h, i: (h, 0)),
            pl.BlockSpec((nb, hd), lambda h, i: (0, h)),
            pl.BlockSpec((None, 2, tq, 2 * tq), lambda h, i: (h, 0, 0, 0)),
            _resident(e.shape),
            _resident(place.shape),
        ],
        out_specs=pl.BlockSpec((tq, 2 * hd), lambda h, i: (i, h)),
        scratch_shapes=[pltpu.VMEM((2 * hd, 2 * tq), BF16), pltpu.VMEM(((nchunk + 1) * BF16_ROWS, 2 * tq), BF16),
                        pltpu.VMEM((1, 2 * tq), F32), pltpu.VMEM((hv, 2 * tq), F32),
                        pltpu.VMEM((cb, 2 * tq), F32), pltpu.VMEM((cb, 2 * tq), F32)],
        compiler_params=_cparams(("arbitrary", "arbitrary")),
        name="moba_prompt",
    )(q, kb, vt, km, near, e, place)


def _paged(layer, p, npg):
    def index(b, g, pt):
        return (layer, pt[b, g * npg + p], 0, 0)
    return index


def _pv_ones(p, v, ones):
    return _dot(p.astype(BF16), jnp.concatenate([v.astype(BF16), ones[:v.shape[0]]], axis=1))


def _sample_blocks_kernel(pt_ref, q_ref, base_ref, last_ref, *refs, nkv, ppb):
    npg = (len(refs) - 2) // 2
    kpages, vpages = refs[:npg], refs[npg:2 * npg]
    part_ref, km_ref = refs[2 * npg:]
    is_last = pl.program_id(1) == pl.num_programs(1) - 1
    rows, hd = q_ref.shape
    prow = kpages[0].shape[0]
    qb = (q_ref[...] * LOG2E).astype(BF16)
    ones = jnp.ones((prow, hd), BF16)
    lane = lax.broadcasted_iota(jnp.int32, (rows, 2 * hd), 1)
    bps = npg // ppb
    for b in range(bps):
        tot = jnp.zeros((SUBLANES, hd), F32)
        ss = []
        for r in range(ppb):
            kp = kpages[b * ppb + r][...]
            tot = tot + jnp.sum(kp.reshape(prow // SUBLANES, SUBLANES, hd), axis=0)
            tile = base_ref[...]
            if b == bps - 1:
                tile = jnp.where(is_last, last_ref[r], tile)
            ss.append(_dot_nt(qb, kp.astype(BF16)) + tile)
        tot = tot + pltpu.roll(tot, nkv, 0)
        km_ref[b] = tot * (1.0 / (ppb * (prow // nkv)))
        mx = ss[0]
        for s in ss[1:]:
            mx = jnp.maximum(mx, s)
        m = jnp.max(mx, axis=1, keepdims=True)
        part = _pv_ones(jnp.exp2(ss[0] - m), vpages[b * ppb][...], ones)
        for r in range(1, ppb):
            part = part + _pv_ones(jnp.exp2(ss[r] - m), vpages[b * ppb + r][...], ones)
        part_ref[b] = jnp.where(lane == 2 * hd - 1, m, part)


def _sample_blocks_call(qall, base, last, ck, cv, page_table, layer, nb, nkv):
    _, _, prow, hd = ck.shape
    bs, npages = page_table.shape
    ppb = npages // nb
    npg = PAGES_PER_STEP
    assert 2 * nkv == SUBLANES and npg % ppb == 0 and npages % npg == 0
    bps = npg // ppb
    rows = qall.shape[1]
    page_spec = lambda p: pl.BlockSpec((None, None, prow, hd), _paged(layer, p, npg))
    const = lambda shape: pl.BlockSpec(shape, lambda b, g, pt: (0,) * len(shape))
    grid_spec = pltpu.PrefetchScalarGridSpec(
        num_scalar_prefetch=1,
        grid=(bs, npages // npg),
        in_specs=[pl.BlockSpec((None, rows, hd), lambda b, g, pt: (b, 0, 0)), const(base.shape), const(last.shape)]
        + [page_spec(p) for p in range(npg)] * 2,
        out_specs=[pl.BlockSpec((None, bps, rows, 2 * hd), lambda b, g, pt: (b, g, 0, 0)),
                   pl.BlockSpec((None, bps, SUBLANES, hd), lambda b, g, pt: (b, g, 0, 0))],
    )
    return pl.pallas_call(
        functools.partial(_sample_blocks_kernel, nkv=nkv, ppb=ppb),
        out_shape=[jax.ShapeDtypeStruct((bs, nb, rows, 2 * hd), F32),
                   jax.ShapeDtypeStruct((bs, nb, SUBLANES, hd), F32)],
        grid_spec=grid_spec,
        compiler_params=_cparams(("parallel", "parallel")),
        name="sample_blocks",
    )(page_table, qall, base, last, *([ck] * npg), *([cv] * npg))


def _sample_merge_kernel(q_ref, kn_ref, vn_ref, km_ref, own_ref, part_ref, o_ref, *, nkv):
    rows, hd = q_ref.shape
    rph = rows // nkv
    nb = km_ref.shape[1]
    q = q_ref[...]
    qb = (q * LOG2E).astype(BF16)
    sel = jnp.concatenate(
        [_select_blocks(_dot_nt_f32(q[h * rph:(h + 1) * rph], km_ref[h]), nb, 1) for h in range(nkv)], axis=0)
    pad = jnp.zeros((LANES - kn_ref.shape[0], hd), F32)
    s = _dot_nt(qb, jnp.concatenate([kn_ref[...], pad], axis=0).astype(BF16)) + own_ref[...]
    m0 = jnp.max(s, axis=1, keepdims=True)
    acc0 = _pv_ones(jnp.exp2(s - m0), jnp.concatenate([vn_ref[...], pad], axis=0), jnp.ones((LANES, hd), BF16))
    lane = lax.broadcasted_iota(jnp.int32, (rows, nb), 1)
    mb = jnp.zeros((rows, nb), F32)
    for j in range(nb):
        mb = jnp.where(lane == j, part_ref[j, :, 2 * hd - 1:2 * hd], mb)
    picked = sel > 0.5
    m = jnp.maximum(m0, jnp.max(jnp.where(picked, mb, NEG), axis=1, keepdims=True))
    w = jnp.where(picked, jnp.exp2(mb - m), 0.0)
    acc = jnp.exp2(m0 - m) * acc0
    for j in range(nb):
        acc = acc + w[:, j:j + 1] * part_ref[j]
    o_ref[...] = acc[:, :hd] / acc[:, hd:hd + 1]


def _sample_merge_call(qall, kn, vn, km, own, parts, nkv):
    bs, rows, hd = qall.shape
    nb = km.shape[2]
    nown = kn.shape[1]
    assert nown <= LANES
    per_seq = lambda *shape: pl.BlockSpec((None,) + shape, lambda b: (b,) + (0,) * len(shape))
    return pl.pallas_call(
        functools.partial(_sample_merge_kernel, nkv=nkv),
        out_shape=jax.ShapeDtypeStruct(qall.shape, F32),
        grid=(bs,),
        in_specs=[per_seq(rows, hd), per_seq(nown, hd), per_seq(nown, hd), per_seq(nkv, nb, hd),
                  pl.BlockSpec(own.shape, lambda b: (0, 0)), per_seq(nb, rows, 2 * hd)],
        out_specs=per_seq(rows, hd),
        compiler_params=_cparams(("parallel",)),
        name="sample_merge",
    )(qall, kn, vn, km, own, parts)


def _rel_bucket(dist, n_buckets):
    n = jnp.maximum(dist, 0)
    max_exact = n_buckets // 2
    nf = jnp.maximum(n, 1).astype(F32)
    large = max_exact + (jnp.log(nf / max_exact) / math.log(MAX_DISTANCE / max_exact)
                         * (n_buckets - max_exact)).astype(jnp.int32)
    large = jnp.minimum(large, n_buckets - 1)
    return jnp.where(n < max_exact, n, large)


def _shifted_bias(dist, rel_table):
    nbk = rel_table.shape[0]
    far = rel_table[_rel_bucket(jnp.asarray(2 * MOBA_BLOCK), nbk)]
    b = (rel_table[_rel_bucket(dist, nbk)] - far) * LOG2E
    return jnp.where((dist >= 0)[..., None], b, NEG)


def _skew_toeplitz(u, n):
    lead = u.shape[:-1]
    up = jnp.concatenate([u, jnp.zeros(lead + (1,), u.dtype)], axis=-1)
    flat = jnp.broadcast_to(up[..., None, :], lead + (n, 2 * n)).reshape(lead + (2 * n * n,))
    return flat[..., :n * (2 * n - 1)].reshape(lead + (n, 2 * n - 1))


def _prompt_near_tiles(rel_table, nkv):
    blk = MOBA_BLOCK
    n = 2 * blk
    u = _shifted_bias(jnp.arange(2 * n - 1) - (blk - 1), rel_table).T
    a = _skew_toeplitz(u, n)[:, :, n - 1:n - 1 + blk]
    a = a.reshape(nkv, 2, 2, blk, blk)
    return jnp.transpose(a, (0, 2, 3, 1, 4)).reshape(nkv, 2, blk, 2 * blk)


def _sample_tiles(rel_table, nkv, lq, psz, ppb):
    rows = 2 * nkv * lq
    row_kvh = jnp.arange(rows) // (2 * lq)
    row_t = jnp.arange(rows) % lq
    col_h = jnp.arange(psz * nkv) % nkv
    base = jnp.where(row_kvh[:, None] == col_h[None, :], 0.0, NEG).astype(F32)
    kk = jnp.arange(ppb * psz)
    b = _shifted_bias(jnp.arange(lq)[:, None] + ppb * psz - kk[None, :], rel_table)
    b = jnp.transpose(b, (2, 0, 1)).reshape(rows, ppb, psz)
    last = jnp.transpose(b, (1, 0, 2))[:, :, :, None] + base.reshape(1, rows, psz, nkv)
    last = last.reshape(ppb, rows, psz * nkv)
    ncol = LANES
    col_t, col_hh = jnp.arange(ncol) // nkv, jnp.arange(ncol) % nkv
    ob = _shifted_bias(jnp.arange(lq)[:, None] - col_t[None, :], rel_table)
    ob = jnp.transpose(ob, (2, 0, 1)).reshape(rows, ncol)
    ok = (row_kvh[:, None] == col_hh[None, :]) & (col_t[None, :] < lq)
    own = jnp.where(ok, ob, NEG)
    return base, last, own


def _block_diag_gates(w_a, w_x, group):
    n, c, _ = w_a.shape
    eye = jnp.eye(group, dtype=w_a.dtype)

    def bd(w):
        w = w.reshape(n // group, group, c, c)
        return jnp.einsum('gbcd,be->gbced', w, eye).reshape(n // group, group * c, group * c)

    return jnp.concatenate([bd(w_a), bd(w_x)], axis=-1).astype(BF16)


def kernel(x_prompt, x_sample, c_prompt, c_sample, cache_k, cache_v, page_table, state_conv, state_rglru, norm_g,
           w_ada, b_ada, w_in, conv_w, conv_b, w_rg_a, b_rg_a, w_rg_x, b_rg_x, lru_lambda, q_norm_g, k_norm_g,
           rel_table, w_out, w_ffn_in, w_ffn_out):
    depth = norm_g.shape[0]
    bp, s, d = x_prompt.shape
    bs, lq, _ = x_sample.shape
    _, npool, psz, nkv, hd = cache_k.shape
    nh = rel_table.shape[1]
    dr = conv_w.shape[2]
    cwid = conv_w.shape[1]
    past_len = page_table.shape[1] * psz
    nb_past = past_len // MOBA_BLOCK
    assert bp == 1 and s % MOBA_BLOCK == 0 and lq == SUBLANES and nh == 2 * nkv
    assert past_len == nb_past * MOBA_BLOCK and MOBA_BLOCK % psz == 0 and cwid - 1 <= lq
    assert MAX_DISTANCE <= MOBA_BLOCK
    dkv = nkv * hd
    ppb = MOBA_BLOCK // psz
    tm_p = 512 if s % 512 == 0 else MOBA_BLOCK
    ts = bs * lq

    c_all = jnp.concatenate([c_prompt, c_sample], axis=0)
    nrow = -(-c_all.shape[0] // SUBLANES) * SUBLANES
    c_all = jnp.pad(c_all, ((0, nrow - c_all.shape[0]), (0, 0)))
    mods = _ada_call(c_all, w_ada, b_ada)

    ck = cache_k.reshape(depth, npool, psz * nkv, hd)
    cv = cache_v.reshape(depth, npool, psz * nkv, hd)
    near_p = _prompt_near_tiles(rel_table, nkv)
    base_s, last_s, own_s = _sample_tiles(rel_table, nkv, lq, psz, ppb)

    xp = x_prompt.reshape(s, d)
    xs = x_sample.reshape(ts, d)
    outs = {k: [] for k in ('kp', 'vp', 'cp', 'hp', 'ks', 'vs', 'cs', 'hs')}
    for l in range(depth):
        mod_p = mods[l, 0:bp]
        mod_s = jnp.repeat(mods[l, bp:bp + bs], lq, axis=0)
        wup0, wup1 = w_ffn_in[l, 0].astype(BF16), w_ffn_in[l, 1].astype(BF16)
        wdn0, wdn1 = w_ffn_out[l, 0].astype(BF16), w_ffn_out[l, 1].astype(BF16)
        win = w_in[l].astype(BF16)
        wout = w_out[l].astype(BF16)
        wg = _block_diag_gates(w_rg_a[l], w_rg_x[l], 4)
        rg = (conv_w[l], conv_b[l], wg, b_rg_a[l], b_rg_x[l], lru_lambda[l])
        proj = dict(dr=dr, nh=nh, nkv=nkv, hd=hd)

        xp = _ffn_call(xp, mod_p, 0, norm_g[l, 0], wup0, wdn0, tm_p)
        xr, gg, sgb, q, k, v, kb, vt, km = _inproj_call(xp, mod_p, norm_g[l, 1], win, q_norm_g[l], k_norm_g[l], tm_p,
                                                        prompt=True, **proj)
        ya, hlast = _rglru_prompt_call(xr, gg, *rg, tm_p)
        yatt = _attn_prompt_call(q, kb, vt, km.reshape(s // MOBA_BLOCK, dkv), near_p, nkv, hd)
        xp = _outproj_ffn_call(xp, ya, sgb, yatt, wout, mod_p, 1, 2, norm_g[l, 2], wup1, wdn1, tm_p)
        outs['kp'].append(k.reshape(bp, s, nkv, hd))
        outs['vp'].append(v.reshape(bp, s, nkv, hd))
        outs['cp'].append(xr[s - (cwid - 1):].reshape(bp, cwid - 1, dr))
        outs['hp'].append(hlast)

        xs = _ffn_call(xs, mod_s, 0, norm_g[l, 0], wup0, wdn0, ts)
        xr, gg, sgb, q, k, v = _inproj_call(xs, mod_s, norm_g[l, 1], win, q_norm_g[l], k_norm_g[l], ts,
                                            prompt=False, **proj)
        xr3 = xr.reshape(bs, lq, dr)
        xcat = jnp.concatenate([state_conv[l], xr3, jnp.zeros((bs, 2 * SUBLANES - lq - (cwid - 1), dr), F32)], axis=1)
        ya, h_all = _rglru_sample_call(xcat, gg, state_rglru[l], *rg)
        qall = jnp.transpose(q.reshape(bs, lq, nkv, 2, hd), (0, 2, 3, 1, 4)).reshape(bs, 2 * nkv * lq, hd)
        parts, km_s = _sample_blocks_call(qall, base_s, last_s, ck, cv, page_table, l, nb_past, nkv)
        km_s = jnp.transpose(km_s[:, :, :nkv], (0, 2, 1, 3))
        yatt = _sample_merge_call(qall, k.reshape(bs, lq * nkv, hd), v.reshape(bs, lq * nkv, hd), km_s,
                                  own_s, parts, nkv)
        yatt = jnp.transpose(yatt.reshape(bs, nkv, 2, lq, hd), (0, 3, 1, 2, 4)).reshape(ts, nh * hd)
        xs = _outproj_ffn_call(xs, ya, sgb, yatt, wout, mod_s, 1, 2, norm_g[l, 2], wup1, wdn1, ts)
        outs['ks'].append(k.reshape(bs, lq, nkv, hd))
        outs['vs'].append(v.reshape(bs, lq, nkv, hd))
        outs['cs'].append(xr3[:, lq - (cwid - 1):])
        outs['hs'].append(h_all.reshape(bs, lq, dr)[:, lq - 1])

    st = {k: jnp.stack(v) for k, v in outs.items()}
    return (xp.reshape(bp, s, d), xs.reshape(bs, lq, d), st['kp'], st['vp'], st['cp'], st['hp'],
            st['ks'], st['vs'], st['cs'], st['hs'])
```

```python
import functools
import math

import jax
import jax.numpy as jnp
from jax import lax
from jax.experimental import pallas as pl
from jax.experimental.pallas import tpu as pltpu

F32 = jnp.float32
BF16 = jnp.bfloat16

EPS = 1e-6
LRU_C = 8.0
MOBA_BLOCK = 256
MOBA_TOP_K = 3
MAX_DISTANCE = 128
NEG = -1e30
LOG2E = 1.4426950408889634

V7X_VMEM_BYTES = 64 * 1024 * 1024
VMEM_LIMIT = V7X_VMEM_BYTES * 3 // 4
SUBLANES = 8
LANES = 128
BF16_ROWS = 16
PAGES_PER_STEP = 16
FAR_BLOCKS = 4


def _cparams(sem):
    return pltpu.CompilerParams(dimension_semantics=sem, vmem_limit_bytes=VMEM_LIMIT)


def _resident(shape):
    nd = len(shape)
    return pl.BlockSpec(shape, lambda *_: (0,) * nd, pipeline_mode=pl.Buffered(1))


def _sigmoid(x):
    return 1.0 / (1.0 + jnp.exp(-x))


def _gelu_tanh(x):
    return 0.5 * x * (1.0 + jnp.tanh(math.sqrt(2.0 / math.pi) * (x + 0.044715 * (x * x * x))))


def _rms_mod(x, g, shift, scale):
    ms = jnp.mean(x * x, axis=-1, keepdims=True)
    y = x * lax.rsqrt(ms + EPS) * g
    return y * (1.0 + scale) + shift


def _dot(a, b):
    return jnp.dot(a, b, preferred_element_type=F32)


def _dot_nt(a, b):
    return lax.dot_general(a, b, (((1,), (1,)), ((), ())), preferred_element_type=F32)


def _dot_f32(a, b):
    return jnp.dot(a, b, preferred_element_type=F32, precision=lax.Precision.HIGHEST)


def _dot_nt_f32(a, b):
    return lax.dot_general(a, b, (((1,), (1,)), ((), ())), preferred_element_type=F32,
                           precision=lax.Precision.HIGHEST)


def _ada_kernel(c_ref, w_ref, b_ref, o_ref):
    c = c_ref[...]
    o_ref[...] = _dot_f32(c * _sigmoid(c), w_ref[...]) + b_ref[...]


def _ada_call(c_all, w_ada, b_ada):
    depth, d, n = w_ada.shape
    rows = c_all.shape[0]
    tn = 1024
    return pl.pallas_call(
        _ada_kernel,
        out_shape=jax.ShapeDtypeStruct((depth, rows, n), F32),
        grid=(depth, n // tn),
        in_specs=[
            pl.BlockSpec((rows, d), lambda l, j: (0, 0)),
            pl.BlockSpec((None, d, tn), lambda l, j: (l, 0, j)),
            pl.BlockSpec((None, 1, tn), lambda l, j: (l, 0, j)),
        ],
        out_specs=pl.BlockSpec((None, rows, tn), lambda l, j: (l, 0, j)),
        compiler_params=_cparams(("arbitrary", "arbitrary")),
        name="adaln",
    )(c_all, w_ada, b_ada.reshape(depth, 1, n))


def _mod_spec(mod, tm, sub, d):
    if mod.shape[0] == 1:
        return pl.BlockSpec((1, 3 * d), lambda i: (0, sub))
    return pl.BlockSpec((tm, 3 * d), lambda i: (i, sub))


def _ffn_body(x, mod_ref, g_ref, wup_ref, wdn_ref, fc):
    d = x.shape[-1]
    f = wdn_ref.shape[0]
    h = _rms_mod(x, g_ref[...], mod_ref[:, 0:d], mod_ref[:, d:2 * d]).astype(BF16)
    acc = jnp.zeros(x.shape, F32)
    for c in range(f // fc):
        a = _dot(h, wup_ref[:, c * fc:(c + 1) * fc])
        g = _dot(h, wup_ref[:, f + c * fc:f + (c + 1) * fc])
        act = (g * _sigmoid(g) * a).astype(BF16)
        acc = acc + _dot(act, wdn_ref[c * fc:(c + 1) * fc, :])
    return x + 0.5 * mod_ref[:, 2 * d:3 * d] * acc


def _ffn_kernel(x_ref, mod_ref, g_ref, wup_ref, wdn_ref, o_ref, *, fc):
    o_ref[...] = _ffn_body(x_ref[...], mod_ref, g_ref, wup_ref, wdn_ref, fc)


def _outproj_ffn_kernel(x_ref, ya_ref, sgb_ref, yatt_ref, wout_ref, modo_ref, mod_ref, g_ref, wup_ref, wdn_ref,
                        o_ref, *, fc):
    x = x_ref[...]
    d = x.shape[-1]
    merged = (ya_ref[...] + sgb_ref[...] * yatt_ref[...]).astype(BF16)
    x = x + modo_ref[:, 2 * d:3 * d] * _dot(merged, wout_ref[...])
    o_ref[...] = _ffn_body(x, mod_ref, g_ref, wup_ref, wdn_ref, fc)


def _ffn_chunk(f):
    for fc in (512, 256, 128):
        if f % fc == 0:
            return fc
    return f


def _ffn_call(x, mod, sub, g, wup, wdn, tm):
    t, d = x.shape
    row = pl.BlockSpec((tm, d), lambda i: (i, 0))
    return pl.pallas_call(
        functools.partial(_ffn_kernel, fc=_ffn_chunk(wdn.shape[0])),
        out_shape=jax.ShapeDtypeStruct((t, d), F32),
        grid=(t // tm,),
        in_specs=[row, _mod_spec(mod, tm, sub, d), _resident((1, d)), _resident(wup.shape), _resident(wdn.shape)],
        out_specs=row,
        compiler_params=_cparams(("parallel",)),
        name="ffn",
    )(x, mod, g.reshape(1, d), wup, wdn)


def _outproj_ffn_call(x, ya, sgb, yatt, wout, mod, sub_out, sub, g, wup, wdn, tm):
    t, d = x.shape
    row = pl.BlockSpec((tm, d), lambda i: (i, 0))
    return pl.pallas_call(
        functools.partial(_outproj_ffn_kernel, fc=_ffn_chunk(wdn.shape[0])),
        out_shape=jax.ShapeDtypeStruct((t, d), F32),
        grid=(t // tm,),
        in_specs=[row, row, row, row, _resident(wout.shape), _mod_spec(mod, tm, sub_out, d),
                  _mod_spec(mod, tm, sub, d), _resident((1, d)), _resident(wup.shape), _resident(wdn.shape)],
        out_specs=row,
        compiler_params=_cparams(("parallel",)),
        name="outproj_ffn",
    )(x, ya, sgb, yatt, wout, mod, mod, g.reshape(1, d), wup, wdn)


def _head_rms(u, g, nheads, hd, scale):
    outs = []
    for h in range(nheads):
        uh = u[:, h * hd:(h + 1) * hd]
        ms = jnp.mean(uh * uh, axis=-1, keepdims=True)
        outs.append(uh * lax.rsqrt(ms + EPS) * g * scale)
    return outs


def _inproj_kernel(x_ref, mod_ref, g_ref, win_ref, qg_ref, kg_ref, *out_refs, dr, nh, nkv, hd, prompt):
    if prompt:
        xr_ref, gg_ref, sgb_ref, q_ref, k_ref, v_ref, kb_ref, vt_ref, km_ref = out_refs
    else:
        xr_ref, gg_ref, sgb_ref, q_ref, k_ref, v_ref = out_refs
    x = x_ref[...]
    tm, d = x.shape
    h = _rms_mod(x, g_ref[...], mod_ref[:, 0:d], mod_ref[:, d:2 * d]).astype(BF16)
    dq, dkv = nh * hd, nkv * hd
    o_q = 2 * dr
    o_k = o_q + dq
    o_v = o_k + dkv
    o_ga = o_v + dkv
    o_gb = o_ga + d

    def seg(lo, width):
        return _dot(h, win_ref[:, lo:lo + width])

    xr_ref[...] = seg(0, dr)
    gg_ref[...] = _sigmoid(seg(o_ga, d)) * _gelu_tanh(seg(dr, dr))
    sgb_ref[...] = _sigmoid(seg(o_gb, d))
    qs = _head_rms(seg(o_q, dq), qg_ref[...], nh, hd, hd ** -0.5)
    for hh in range(nh):
        q_ref[:, hh * hd:(hh + 1) * hd] = qs[hh]
    ks = _head_rms(seg(o_k, dkv), kg_ref[...], nkv, hd, 1.0)
    v = seg(o_v, dkv)
    for hh in range(nkv):
        k_ref[pl.ds(hh, tm, stride=nkv), :] = ks[hh]
        v_ref[pl.ds(hh, tm, stride=nkv), :] = v[:, hh * hd:(hh + 1) * hd]
    if prompt:
        vt = v.T.astype(BF16)
        hv = hd + BF16_ROWS
        for hh in range(nkv):
            vt_ref[hh * hv:hh * hv + hd, :] = vt[hh * hd:(hh + 1) * hd, :]
            vt_ref[hh * hv + hd:(hh + 1) * hv, :] = jnp.ones((BF16_ROWS, tm), BF16)
            kb_ref[:, hh * hd:(hh + 1) * hd] = ks[hh].astype(BF16)
            for b in range(tm // MOBA_BLOCK):
                km_ref[b, :, hh * hd:(hh + 1) * hd] = jnp.mean(
                    ks[hh][b * MOBA_BLOCK:(b + 1) * MOBA_BLOCK], axis=0, keepdims=True)


def _inproj_call(x, mod, g, win, qg, kg, tm, *, dr, nh, nkv, hd, prompt):
    t, d = x.shape
    dq, dkv = nh * hd, nkv * hd
    row = lambda w: pl.BlockSpec((tm, w), lambda i: (i, 0))
    out_shape = [jax.ShapeDtypeStruct((t, w), F32) for w in (dr, d, d, dq)]
    out_specs = [row(w) for w in (dr, d, d, dq)]
    out_shape += [jax.ShapeDtypeStruct((t * nkv, hd), F32)] * 2
    out_specs += [pl.BlockSpec((tm * nkv, hd), lambda i: (i, 0))] * 2
    if prompt:
        nbt = tm // MOBA_BLOCK
        dvt = nkv * (hd + BF16_ROWS)
        out_shape += [jax.ShapeDtypeStruct((t, dkv), BF16), jax.ShapeDtypeStruct((dvt, t), BF16),
                      jax.ShapeDtypeStruct((t // MOBA_BLOCK, 1, dkv), F32)]
        out_specs += [row(dkv), pl.BlockSpec((dvt, tm), lambda i: (0, i)),
                      pl.BlockSpec((nbt, 1, dkv), lambda i: (i, 0, 0))]
    return pl.pallas_call(
        functools.partial(_inproj_kernel, dr=dr, nh=nh, nkv=nkv, hd=hd, prompt=prompt),
        out_shape=out_shape,
        grid=(t // tm,),
        in_specs=[row(d), _mod_spec(mod, tm, 1, d), _resident((1, d)), _resident(win.shape),
                  _resident((1, hd)), _resident((1, hd))],
        out_specs=out_specs,
        compiler_params=_cparams(("parallel",)),
        name="inproj",
    )(x, mod, g.reshape(1, d), win, qg.reshape(1, hd), kg.reshape(1, hd))


def _rglru_gates(xc, wg_ref, ba_ref, bx_ref, lam_ref, a_scr, b_scr):
    ngroups, gw, _ = wg_ref.shape
    xcb = xc.astype(BF16)
    for gi in range(ngroups):
        cols = slice(gi * gw, (gi + 1) * gw)
        u = _dot(xcb[:, cols], wg_ref[gi])
        r = _sigmoid(u[:, :gw] + ba_ref[:, cols])
        gate_x = _sigmoid(u[:, gw:] + bx_ref[:, cols])
        lam = lam_ref[:, cols]
        softplus_neg = jnp.maximum(-lam, 0.0) + jnp.log(1.0 + jnp.exp(-jnp.abs(lam)))
        log_a = -LRU_C * r * softplus_neg
        a = jnp.exp(log_a)
        mult = jnp.sqrt(-_expm1(2.0 * log_a))
        a_scr[:, cols] = a
        b_scr[:, cols] = mult * gate_x * xc[:, cols]


def _expm1(x):
    small = x * (1.0 + x * (0.5 + x * (1.0 / 6.0 + x * (1.0 / 24.0 + x * (1.0 / 120.0)))))
    return jnp.where(x > -0.1, small, jnp.exp(x) - 1.0)


def _scan8(a, b):
    row = lax.broadcasted_iota(jnp.int32, a.shape, 0)
    for k in (1, 2, 4):
        keep = row >= k
        a_sh = jnp.where(keep, pltpu.roll(a, k, 0), 1.0)
        b_sh = jnp.where(keep, pltpu.roll(b, k, 0), 0.0)
        b = a * b_sh + b
        a = a * a_sh
    return a, b


def _rglru_prompt_kernel(xr_ref, gg_ref, cw_ref, cb_ref, wg_ref, ba_ref, bx_ref, lam_ref, ya_ref, hlast_ref,
                         xbuf, a_scr, b_scr, hcar):
    i = pl.program_id(0)
    tm, dr = xr_ref.shape
    cwid = cw_ref.shape[0]

    @pl.when(i == 0)
    def _():
        xbuf[0:SUBLANES, :] = jnp.zeros((SUBLANES, dr), F32)
        hcar[...] = jnp.zeros_like(hcar)

    xbuf[SUBLANES:SUBLANES + tm, :] = xr_ref[...]
    xc = cb_ref[...] + jnp.zeros((tm, dr), F32)
    for j in range(cwid):
        off = SUBLANES - (cwid - 1) + j
        xc = xc + cw_ref[j:j + 1, :] * xbuf[off:off + tm, :]
    xbuf[0:SUBLANES, :] = xbuf[tm:tm + SUBLANES, :]
    _rglru_gates(xc, wg_ref, ba_ref, bx_ref, lam_ref, a_scr, b_scr)

    def body(g, carry):
        rows = pl.ds(pl.multiple_of(g * SUBLANES, SUBLANES), SUBLANES)
        a, b = _scan8(a_scr[rows, :], b_scr[rows, :])
        h = a * carry + b
        ya_ref[rows, :] = h * gg_ref[rows, :]
        return h[SUBLANES - 1:SUBLANES, :]

    hc = lax.fori_loop(0, tm // SUBLANES, body, hcar[...])
    hcar[...] = hc
    hlast_ref[...] = hc


def _rglru_prompt_call(xr, gg, cw, cb, wg, ba, bx, lam, tm):
    t, dr = xr.shape
    row = pl.BlockSpec((tm, dr), lambda i: (i, 0))
    vec = _resident((1, dr))
    return pl.pallas_call(
        _rglru_prompt_kernel,
        out_shape=[jax.ShapeDtypeStruct((t, dr), F32), jax.ShapeDtypeStruct((1, dr), F32)],
        grid=(t // tm,),
        in_specs=[row, row, _resident(cw.shape), vec, _resident(wg.shape), vec, vec, vec],
        out_specs=[row, pl.BlockSpec((1, dr), lambda i: (0, 0))],
        scratch_shapes=[pltpu.VMEM((tm + SUBLANES, dr), F32), pltpu.VMEM((tm, dr), F32),
                        pltpu.VMEM((tm, dr), F32), pltpu.VMEM((1, dr), F32)],
        compiler_params=_cparams(("arbitrary",)),
        name="rglru_prompt",
    )(xr, gg, cw, cb.reshape(1, dr), wg, ba.reshape(1, dr), bx.reshape(1, dr), lam.reshape(1, dr))


def _rglru_sample_kernel(xcat_ref, gg_ref, h0_ref, cw_ref, cb_ref, wg_ref, ba_ref, bx_ref, lam_ref, ya_ref, h_ref,
                         xc_scr, a_scr, b_scr):
    nseq, _, dr = xcat_ref.shape
    cwid = cw_ref.shape[0]

    def conv(s, _):
        xc = cb_ref[...] + jnp.zeros((SUBLANES, dr), F32)
        for j in range(cwid):
            xc = xc + cw_ref[j:j + 1, :] * xcat_ref[s, j:j + SUBLANES, :]
        xc_scr[pl.ds(pl.multiple_of(s * SUBLANES, SUBLANES), SUBLANES), :] = xc
        return 0

    lax.fori_loop(0, nseq, conv, 0)
    _rglru_gates(xc_scr[...], wg_ref, ba_ref, bx_ref, lam_ref, a_scr, b_scr)

    def body(s, _):
        rows = pl.ds(pl.multiple_of(s * SUBLANES, SUBLANES), SUBLANES)
        a, b = _scan8(a_scr[rows, :], b_scr[rows, :])
        h = a * h0_ref[s] + b
        h_ref[rows, :] = h
        ya_ref[rows, :] = h * gg_ref[rows, :]
        return 0

    lax.fori_loop(0, nseq, body, 0)


def _rglru_sample_call(xcat, gg, h0, cw, cb, wg, ba, bx, lam):
    nseq, _, dr = xcat.shape
    t = nseq * SUBLANES
    full = lambda shape: pl.BlockSpec(shape, lambda i: (0,) * len(shape))
    return pl.pallas_call(
        _rglru_sample_kernel,
        out_shape=[jax.ShapeDtypeStruct((t, dr), F32), jax.ShapeDtypeStruct((t, dr), F32)],
        grid=(1,),
        in_specs=[full(xcat.shape), full((t, dr)), full((nseq, 1, dr)), full(cw.shape), full((1, dr)),
                  full(wg.shape), full((1, dr)), full((1, dr)), full((1, dr))],
        out_specs=[full((t, dr)), full((t, dr))],
        scratch_shapes=[pltpu.VMEM((t, dr), F32), pltpu.VMEM((t, dr), F32), pltpu.VMEM((t, dr), F32)],
        compiler_params=_cparams(("arbitrary",)),
        name="rglru_sample",
    )(xcat, gg, h0.reshape(nseq, 1, dr), cw, cb.reshape(1, dr), wg, ba.reshape(1, dr), bx.reshape(1, dr),
      lam.reshape(1, dr))


def _select_blocks(gs, n_eligible, axis):
    nb = gs.shape[axis]
    jj = lax.broadcasted_iota(jnp.int32, gs.shape, axis)
    s = jnp.where(jj < n_eligible, gs, -jnp.inf)
    sel = jnp.zeros(gs.shape, F32)
    for p in range(MOBA_TOP_K):
        m = jnp.max(s, axis=axis, keepdims=True)
        idx = jnp.min(jnp.where(s == m, jj, nb), axis=axis, keepdims=True)
        hit = jj == idx
        sel = jnp.where(jnp.logical_and(hit, jj * 0 + p < n_eligible), 1.0, sel)
        s = jnp.where(hit, -jnp.inf, s)
    return sel


def _attn_prompt_kernel(q_ref, k_ref, vt_ref, km_ref, near_ref, e_ref, place_ref, o_ref,
                        qa_scr, fm_scr, m_scr, acc_scr, s0_scr, s1_scr):
    i = pl.program_id(1)
    tq, hd2 = q_ref.shape
    hd = hd2 // 2
    blk = MOBA_BLOCK
    cb = FAR_BLOCKS * blk
    nb = km_ref.shape[0]
    nchunk = nb // FAR_BLOCKS
    q2 = q_ref[...]
    qt = jnp.concatenate([q2[:, :hd].T, q2[:, hd:].T], axis=1)
    sel = _select_blocks(_dot_f32(km_ref[...], qt), i, 0)
    jj = lax.broadcasted_iota(jnp.int32, sel.shape, 0)
    adj_sel = jnp.max(jnp.where(jj == i - 1, sel, 0.0), axis=0, keepdims=True)
    far_sel = jnp.where(jj < i - 1, sel, 0.0).astype(BF16)
    placed = _dot(place_ref[...], far_sel)
    prow = lax.broadcasted_iota(jnp.int32, placed.shape, 0)
    fm = jnp.where(prow % BF16_ROWS < FAR_BLOCKS, jnp.where(placed > 0.5, 0.0, NEG), 0.0)
    fm_scr[0:nchunk * BF16_ROWS, :] = fm.astype(BF16)
    srow = lax.broadcasted_iota(jnp.int32, (BF16_ROWS, 2 * tq), 0)
    fm_scr[nchunk * BF16_ROWS:(nchunk + 1) * BF16_ROWS, :] = jnp.where(srow < FAR_BLOCKS, NEG, 0.0).astype(BF16)
    qa_scr[0:hd, :] = (qt * LOG2E).astype(BF16)
    qa_scr[hd + BF16_ROWS:2 * hd, :] = jnp.zeros((hd - BF16_ROWS, 2 * tq), BF16)
    qtb = qa_scr[0:hd, :]

    ja = pl.multiple_of(jnp.maximum(i - 1, 0) * blk, blk)
    jo = pl.multiple_of(i * blk, blk)
    s_adj = _dot(k_ref[pl.ds(ja, blk), :], qtb) + (near_ref[0] + jnp.where(adj_sel > 0.5, 0.0, NEG))
    s_own = _dot(k_ref[pl.ds(jo, blk), :], qtb) + near_ref[1]
    m = jnp.maximum(jnp.max(s_adj, axis=0, keepdims=True), jnp.max(s_own, axis=0, keepdims=True))
    p_adj = jnp.exp2(s_adj - m).astype(BF16)
    p_own = jnp.exp2(s_own - m).astype(BF16)
    acc_scr[...] = _dot(vt_ref[:, pl.ds(ja, blk)], p_adj) + _dot(vt_ref[:, pl.ds(jo, blk)], p_own)
    m_scr[...] = m

    nch = jnp.maximum(i + FAR_BLOCKS - 2, 0) // FAR_BLOCKS

    def chunk_offset(c):
        return pl.multiple_of(jnp.minimum(c, nchunk - 1) * cb, cb)

    def scores(c, s_ref):
        slab = pl.multiple_of(jnp.where(c < nch, c, nchunk) * BF16_ROWS, BF16_ROWS)
        qa_scr[hd:hd + BF16_ROWS, :] = fm_scr[pl.ds(slab, BF16_ROWS), :]
        kc = jnp.concatenate([k_ref[pl.ds(chunk_offset(c), cb), :], e_ref[...]], axis=1)
        st = _dot(kc, qa_scr[...])
        s_ref[0:cb, :] = st
        s_ref[cb:cb + 1, :] = jnp.max(st, axis=0, keepdims=True)

    def accumulate(c, s_ref):
        m_old = m_scr[...]
        m_new = jnp.maximum(m_old, s_ref[cb:cb + 1, :])
        p = jnp.exp2(s_ref[0:cb, :] - m_new).astype(BF16)
        acc_scr[...] = jnp.exp2(m_old - m_new) * acc_scr[...] + _dot(vt_ref[:, pl.ds(chunk_offset(c), cb)], p)
        m_scr[...] = m_new

    scores(0, s0_scr)

    def far_body(t, _):
        scores(2 * t + 1, s1_scr)
        accumulate(2 * t, s0_scr)
        scores(2 * t + 2, s0_scr)
        accumulate(2 * t + 1, s1_scr)
        return 0

    lax.fori_loop(0, nch // 2, far_body, 0)

    @pl.when(nch % 2 == 1)
    def _():
        accumulate(nch - 1, s0_scr)

    acc = acc_scr[...]
    ot = acc[:hd] / acc[hd:hd + 1]
    o_ref[:, :hd] = ot[:, :tq].T
    o_ref[:, hd:] = ot[:, tq:].T


def _attn_prompt_call(q, kb, vt, km, near, nkv, hd):
    s = q.shape[0]
    tq = MOBA_BLOCK
    nb = s // tq
    assert nb % FAR_BLOCKS == 0 and nb <= hd
    cb = FAR_BLOCKS * tq
    nchunk = nb // FAR_BLOCKS
    hv = hd + BF16_ROWS
    e = (jnp.arange(hd)[None, :] == (jnp.arange(cb) // tq)[:, None]).astype(BF16)
    pr = jnp.arange(nchunk * BF16_ROWS)
    place = ((pr % BF16_ROWS < FAR_BLOCKS)[:, None]
             & ((pr // BF16_ROWS * FAR_BLOCKS + pr % BF16_ROWS)[:, None] == jnp.arange(nb)[None, :])).astype(BF16)
    return pl.pallas_call(
        _attn_prompt_kernel,
        out_shape=jax.ShapeDtypeStruct(q.shape, F32),
        grid=(nkv, nb),
        in_specs=[
            pl.BlockSpec((tq, 2 * hd), lambda h, i: (i, h)),
            pl.BlockSpec((s, hd), lambda h, i: (0, h)),
            pl.BlockSpec((hv, s), lambda h, i: (h, 0)),
            pl.BlockSpec((nb, hd), lambda h, i: (0, h)),
            pl.BlockSpec((None, 2, tq, 2 * tq), lambda h, i: (h, 0, 0, 0)),
            _resident(e.shape),
            _resident(place.shape),
        ],
        out_specs=pl.BlockSpec((tq, 2 * hd), lambda h, i: (i, h)),
        scratch_shapes=[pltpu.VMEM((2 * hd, 2 * tq), BF16), pltpu.VMEM(((nchunk + 1) * BF16_ROWS, 2 * tq), BF16),
                        pltpu.VMEM((1, 2 * tq), F32), pltpu.VMEM((hv, 2 * tq), F32),
                        pltpu.VMEM((cb + SUBLANES, 2 * tq), F32), pltpu.VMEM((cb + SUBLANES, 2 * tq), F32)],
        compiler_params=_cparams(("arbitrary", "arbitrary")),
        name="moba_prompt",
    )(q, kb, vt, km, near, e, place)


def _paged(layer, p, npg):
    def index(b, g, pt):
        return (layer, pt[b, g * npg + p], 0, 0)
    return index


def _pv_ones(p, v, ones):
    return _dot(p.astype(BF16), jnp.concatenate([v.astype(BF16), ones[:v.shape[0]]], axis=1))


def _sample_blocks_kernel(pt_ref, q_ref, base_ref, last_ref, *refs, nkv, ppb):
    npg = (len(refs) - 2) // 2
    kpages, vpages = refs[:npg], refs[npg:2 * npg]
    part_ref, km_ref = refs[2 * npg:]
    is_last = pl.program_id(1) == pl.num_programs(1) - 1
    rows, hd = q_ref.shape
    prow = kpages[0].shape[0]
    qb = (q_ref[...] * LOG2E).astype(BF16)
    ones = jnp.ones((prow, hd), BF16)
    lane = lax.broadcasted_iota(jnp.int32, (rows, 2 * hd), 1)
    bps = npg // ppb
    for b in range(bps):
        tot = jnp.zeros((SUBLANES, hd), F32)
        ss = []
        for r in range(ppb):
            kp = kpages[b * ppb + r][...]
            tot = tot + jnp.sum(kp.reshape(prow // SUBLANES, SUBLANES, hd), axis=0)
            tile = base_ref[...]
            if b == bps - 1:
                tile = jnp.where(is_last, last_ref[r], tile)
            ss.append(_dot_nt(qb, kp.astype(BF16)) + tile)
        tot = tot + pltpu.roll(tot, nkv, 0)
        km_ref[b] = tot * (1.0 / (ppb * (prow // nkv)))
        mx = ss[0]
        for s in ss[1:]:
            mx = jnp.maximum(mx, s)
        m = jnp.max(mx, axis=1, keepdims=True)
        part = _pv_ones(jnp.exp2(ss[0] - m), vpages[b * ppb][...], ones)
        for r in range(1, ppb):
            part = part + _pv_ones(jnp.exp2(ss[r] - m), vpages[b * ppb + r][...], ones)
        part_ref[b] = jnp.where(lane == 2 * hd - 1, m, part)


def _sample_blocks_call(qall, base, last, ck, cv, page_table, layer, nb, nkv):
    _, _, prow, hd = ck.shape
    bs, npages = page_table.shape
    ppb = npages // nb
    npg = PAGES_PER_STEP
    assert 2 * nkv == SUBLANES and npg % ppb == 0 and npages % npg == 0
    bps = npg // ppb
    rows = qall.shape[1]
    page_spec = lambda p: pl.BlockSpec((None, None, prow, hd), _paged(layer, p, npg))
    const = lambda shape: pl.BlockSpec(shape, lambda b, g, pt: (0,) * len(shape))
    grid_spec = pltpu.PrefetchScalarGridSpec(
        num_scalar_prefetch=1,
        grid=(bs, npages // npg),
        in_specs=[pl.BlockSpec((None, rows, hd), lambda b, g, pt: (b, 0, 0)), const(base.shape), const(last.shape)]
        + [page_spec(p) for p in range(npg)] * 2,
        out_specs=[pl.BlockSpec((None, bps, rows, 2 * hd), lambda b, g, pt: (b, g, 0, 0)),
                   pl.BlockSpec((None, bps, SUBLANES, hd), lambda b, g, pt: (b, g, 0, 0))],
    )
    return pl.pallas_call(
        functools.partial(_sample_blocks_kernel, nkv=nkv, ppb=ppb),
        out_shape=[jax.ShapeDtypeStruct((bs, nb, rows, 2 * hd), F32),
                   jax.ShapeDtypeStruct((bs, nb, SUBLANES, hd), F32)],
        grid_spec=grid_spec,
        compiler_params=_cparams(("parallel", "parallel")),
        name="sample_blocks",
    )(page_table, qall, base, last, *([ck] * npg), *([cv] * npg))


def _sample_merge_kernel(q_ref, kn_ref, vn_ref, km_ref, own_ref, part_ref, o_ref, *, nkv):
    rows, hd = q_ref.shape
    rph = rows // nkv
    nb = km_ref.shape[1]
    q = q_ref[...]
    qb = (q * LOG2E).astype(BF16)
    sel = jnp.concatenate(
        [_select_blocks(_dot_nt_f32(q[h * rph:(h + 1) * rph], km_ref[h]), nb, 1) for h in range(nkv)], axis=0)
    pad = jnp.zeros((LANES - kn_ref.shape[0], hd), F32)
    s = _dot_nt(qb, jnp.concatenate([kn_ref[...], pad], axis=0).astype(BF16)) + own_ref[...]
    m0 = jnp.max(s, axis=1, keepdims=True)
    acc0 = _pv_ones(jnp.exp2(s - m0), jnp.concatenate([vn_ref[...], pad], axis=0), jnp.ones((LANES, hd), BF16))
    lane = lax.broadcasted_iota(jnp.int32, (rows, nb), 1)
    mb = jnp.zeros((rows, nb), F32)
    for j in range(nb):
        mb = jnp.where(lane == j, part_ref[j, :, 2 * hd - 1:2 * hd], mb)
    picked = sel > 0.5
    m = jnp.maximum(m0, jnp.max(jnp.where(picked, mb, NEG), axis=1, keepdims=True))
    w = jnp.where(picked, jnp.exp2(mb - m), 0.0)
    acc = jnp.exp2(m0 - m) * acc0
    for j in range(nb):
        acc = acc + w[:, j:j + 1] * part_ref[j]
    o_ref[...] = acc[:, :hd] / acc[:, hd:hd + 1]


def _sample_merge_call(qall, kn, vn, km, own, parts, nkv):
    bs, rows, hd = qall.shape
    nb = km.shape[2]
    nown = kn.shape[1]
    assert nown <= LANES
    per_seq = lambda *shape: pl.BlockSpec((None,) + shape, lambda b: (b,) + (0,) * len(shape))
    return pl.pallas_call(
        functools.partial(_sample_merge_kernel, nkv=nkv),
        out_shape=jax.ShapeDtypeStruct(qall.shape, F32),
        grid=(bs,),
        in_specs=[per_seq(rows, hd), per_seq(nown, hd), per_seq(nown, hd), per_seq(nkv, nb, hd),
                  pl.BlockSpec(own.shape, lambda b: (0, 0)), per_seq(nb, rows, 2 * hd)],
        out_specs=per_seq(rows, hd),
        compiler_params=_cparams(("parallel",)),
        name="sample_merge",
    )(qall, kn, vn, km, own, parts)


def _rel_bucket(dist, n_buckets):
    n = jnp.maximum(dist, 0)
    max_exact = n_buckets // 2
    nf = jnp.maximum(n, 1).astype(F32)
    large = max_exact + (jnp.log(nf / max_exact) / math.log(MAX_DISTANCE / max_exact)
                         * (n_buckets - max_exact)).astype(jnp.int32)
    large = jnp.minimum(large, n_buckets - 1)
    return jnp.where(n < max_exact, n, large)


def _shifted_bias(dist, rel_table):
    nbk = rel_table.shape[0]
    far = rel_table[_rel_bucket(jnp.asarray(2 * MOBA_BLOCK), nbk)]
    b = (rel_table[_rel_bucket(dist, nbk)] - far) * LOG2E
    return jnp.where((dist >= 0)[..., None], b, NEG)


def _skew_toeplitz(u, n):
    lead = u.shape[:-1]
    up = jnp.concatenate([u, jnp.zeros(lead + (1,), u.dtype)], axis=-1)
    flat = jnp.broadcast_to(up[..., None, :], lead + (n, 2 * n)).reshape(lead + (2 * n * n,))
    return flat[..., :n * (2 * n - 1)].reshape(lead + (n, 2 * n - 1))


def _prompt_near_tiles(rel_table, nkv):
    blk = MOBA_BLOCK
    x = jnp.arange(2 * blk - 1) - (blk - 1)
    u = jnp.stack([_shifted_bias(x + blk, rel_table).T, _shifted_bias(x, rel_table).T])
    a = _skew_toeplitz(u, blk)[..., blk - 1:2 * blk - 1]
    a = a.reshape(2, nkv, 2, blk, blk)
    return jnp.transpose(a, (1, 0, 3, 2, 4)).reshape(nkv, 2, blk, 2 * blk)


def _sample_tiles(rel_table, nkv, lq, psz, ppb):
    rows = 2 * nkv * lq
    row_kvh = jnp.arange(rows) // (2 * lq)
    row_t = jnp.arange(rows) % lq
    col_h = jnp.arange(psz * nkv) % nkv
    base = jnp.where(row_kvh[:, None] == col_h[None, :], 0.0, NEG).astype(F32)
    kk = jnp.arange(ppb * psz)
    b = _shifted_bias(jnp.arange(lq)[:, None] + ppb * psz - kk[None, :], rel_table)
    b = jnp.transpose(b, (2, 0, 1)).reshape(rows, ppb, psz)
    last = jnp.transpose(b, (1, 0, 2))[:, :, :, None] + base.reshape(1, rows, psz, nkv)
    last = last.reshape(ppb, rows, psz * nkv)
    ncol = LANES
    col_t, col_hh = jnp.arange(ncol) // nkv, jnp.arange(ncol) % nkv
    ob = _shifted_bias(jnp.arange(lq)[:, None] - col_t[None, :], rel_table)
    ob = jnp.transpose(ob, (2, 0, 1)).reshape(rows, ncol)
    ok = (row_kvh[:, None] == col_hh[None, :]) & (col_t[None, :] < lq)
    own = jnp.where(ok, ob, NEG)
    return base, last, own


def _block_diag_gates(w_a, w_x, group):
    n, c, _ = w_a.shape
    eye = jnp.eye(group, dtype=w_a.dtype)

    def bd(w):
        w = w.reshape(n // group, group, c, c)
        return jnp.einsum('gbcd,be->gbced', w, eye).reshape(n // group, group * c, group * c)

    return jnp.concatenate([bd(w_a), bd(w_x)], axis=-1).astype(BF16)


def kernel(x_prompt, x_sample, c_prompt, c_sample, cache_k, cache_v, page_table, state_conv, state_rglru, norm_g,
           w_ada, b_ada, w_in, conv_w, conv_b, w_rg_a, b_rg_a, w_rg_x, b_rg_x, lru_lambda, q_norm_g, k_norm_g,
           rel_table, w_out, w_ffn_in, w_ffn_out):
    depth = norm_g.shape[0]
    bp, s, d = x_prompt.shape
    bs, lq, _ = x_sample.shape
    _, npool, psz, nkv, hd = cache_k.shape
    nh = rel_table.shape[1]
    dr = conv_w.shape[2]
    cwid = conv_w.shape[1]
    past_len = page_table.shape[1] * psz
    nb_past = past_len // MOBA_BLOCK
    assert bp == 1 and s % MOBA_BLOCK == 0 and lq == SUBLANES and nh == 2 * nkv
    assert past_len == nb_past * MOBA_BLOCK and MOBA_BLOCK % psz == 0 and cwid - 1 <= lq
    assert MAX_DISTANCE <= MOBA_BLOCK
    dkv = nkv * hd
    ppb = MOBA_BLOCK // psz
    tm_p = 512 if s % 512 == 0 else MOBA_BLOCK
    ts = bs * lq

    c_all = jnp.concatenate([c_prompt, c_sample], axis=0)
    nrow = -(-c_all.shape[0] // SUBLANES) * SUBLANES
    c_all = jnp.pad(c_all, ((0, nrow - c_all.shape[0]), (0, 0)))
    mods = _ada_call(c_all, w_ada, b_ada)

    ck = cache_k.reshape(depth, npool, psz * nkv, hd)
    cv = cache_v.reshape(depth, npool, psz * nkv, hd)
    near_p = _prompt_near_tiles(rel_table, nkv)
    base_s, last_s, own_s = _sample_tiles(rel_table, nkv, lq, psz, ppb)

    xp = x_prompt.reshape(s, d)
    xs = x_sample.reshape(ts, d)
    outs = {k: [] for k in ('kp', 'vp', 'cp', 'hp', 'ks', 'vs', 'cs', 'hs')}
    for l in range(depth):
        mod_p = mods[l, 0:bp]
        mod_s = jnp.repeat(mods[l, bp:bp + bs], lq, axis=0)
        wup0, wup1 = w_ffn_in[l, 0].astype(BF16), w_ffn_in[l, 1].astype(BF16)
        wdn0, wdn1 = w_ffn_out[l, 0].astype(BF16), w_ffn_out[l, 1].astype(BF16)
        win = w_in[l].astype(BF16)
        wout = w_out[l].astype(BF16)
        wg = _block_diag_gates(w_rg_a[l], w_rg_x[l], 4)
        rg = (conv_w[l], conv_b[l], wg, b_rg_a[l], b_rg_x[l], lru_lambda[l])
        proj = dict(dr=dr, nh=nh, nkv=nkv, hd=hd)

        xp = _ffn_call(xp, mod_p, 0, norm_g[l, 0], wup0, wdn0, tm_p)
        xr, gg, sgb, q, k, v, kb, vt, km = _inproj_call(xp, mod_p, norm_g[l, 1], win, q_norm_g[l], k_norm_g[l], tm_p,
                                                        prompt=True, **proj)
        ya, hlast = _rglru_prompt_call(xr, gg, *rg, tm_p)
        yatt = _attn_prompt_call(q, kb, vt, km.reshape(s // MOBA_BLOCK, dkv), near_p, nkv, hd)
        xp = _outproj_ffn_call(xp, ya, sgb, yatt, wout, mod_p, 1, 2, norm_g[l, 2], wup1, wdn1, tm_p)
        outs['kp'].append(k.reshape(bp, s, nkv, hd))
        outs['vp'].append(v.reshape(bp, s, nkv, hd))
        outs['cp'].append(xr[s - (cwid - 1):].reshape(bp, cwid - 1, dr))
        outs['hp'].append(hlast)

        xs = _ffn_call(xs, mod_s, 0, norm_g[l, 0], wup0, wdn0, ts)
        xr, gg, sgb, q, k, v = _inproj_call(xs, mod_s, norm_g[l, 1], win, q_norm_g[l], k_norm_g[l], ts,
                                            prompt=False, **proj)
        xr3 = xr.reshape(bs, lq, dr)
        xcat = jnp.concatenate([state_conv[l], xr3, jnp.zeros((bs, 2 * SUBLANES - lq - (cwid - 1), dr), F32)], axis=1)
        ya, h_all = _rglru_sample_call(xcat, gg, state_rglru[l], *rg)
        qall = jnp.transpose(q.reshape(bs, lq, nkv, 2, hd), (0, 2, 3, 1, 4)).reshape(bs, 2 * nkv * lq, hd)
        parts, km_s = _sample_blocks_call(qall, base_s, last_s, ck, cv, page_table, l, nb_past, nkv)
        km_s = jnp.transpose(km_s[:, :, :nkv], (0, 2, 1, 3))
        yatt = _sample_merge_call(qall, k.reshape(bs, lq * nkv, hd), v.reshape(bs, lq * nkv, hd), km_s,
                                  own_s, parts, nkv)
        yatt = jnp.transpose(yatt.reshape(bs, nkv, 2, lq, hd), (0, 3, 1, 2, 4)).reshape(ts, nh * hd)
        xs = _outproj_ffn_call(xs, ya, sgb, yatt, wout, mod_s, 1, 2, norm_g[l, 2], wup1, wdn1, ts)
        outs['ks'].append(k.reshape(bs, lq, nkv, hd))
        outs['vs'].append(v.reshape(bs, lq, nkv, hd))
        outs['cs'].append(xr3[:, lq - (cwid - 1):])
        outs['hs'].append(h_all.reshape(bs, lq, dr)[:, lq - 1])

    st = {k: jnp.stack(v) for k, v in outs.items()}
    return (xp.reshape(bp, s, d), xs.reshape(bs, lq, d), st['kp'], st['vp'], st['cp'], st['hp'],
            st['ks'], st['vs'], st['cs'], st['hs'])
```

```python
import functools
import math

import jax
import jax.numpy as jnp
from jax import lax
from jax.experimental import pallas as pl
from jax.experimental.pallas import tpu as pltpu

F32 = jnp.float32
BF16 = jnp.bfloat16

EPS = 1e-6
LRU_C = 8.0
MOBA_BLOCK = 256
MOBA_TOP_K = 3
MAX_DISTANCE = 128
NEG = -1e30
LOG2E = 1.4426950408889634

V7X_VMEM_BYTES = 64 * 1024 * 1024
VMEM_LIMIT = V7X_VMEM_BYTES * 3 // 4
SUBLANES = 8
LANES = 128
BF16_ROWS = 16
PAGES_PER_STEP = 16
FAR_BLOCKS = 4


def _cparams(sem):
    return pltpu.CompilerParams(dimension_semantics=sem, vmem_limit_bytes=VMEM_LIMIT)


def _resident(shape):
    nd = len(shape)
    return pl.BlockSpec(shape, lambda *_: (0,) * nd, pipeline_mode=pl.Buffered(1))


def _stacked_weight_spec(w):
    arr, lead = w
    tail = arr.shape[len(lead):]
    return pl.BlockSpec((None,) * len(lead) + tail, lambda *_: lead + (0,) * len(tail),
                        pipeline_mode=pl.Buffered(1))


def _sigmoid(x):
    return 1.0 / (1.0 + jnp.exp(-x))


def _gelu_tanh(x):
    return 0.5 * x * (1.0 + jnp.tanh(math.sqrt(2.0 / math.pi) * (x + 0.044715 * (x * x * x))))


def _rms_mod(x, g, shift, scale):
    ms = jnp.mean(x * x, axis=-1, keepdims=True)
    y = x * lax.rsqrt(ms + EPS) * g
    return y * (1.0 + scale) + shift


def _dot(a, b):
    return jnp.dot(a, b, preferred_element_type=F32)


def _dot_nt(a, b):
    return lax.dot_general(a, b, (((1,), (1,)), ((), ())), preferred_element_type=F32)


def _dot_f32(a, b):
    return jnp.dot(a, b, preferred_element_type=F32, precision=lax.Precision.HIGHEST)


def _dot_nt_f32(a, b):
    return lax.dot_general(a, b, (((1,), (1,)), ((), ())), preferred_element_type=F32,
                           precision=lax.Precision.HIGHEST)


def _ada_kernel(c_ref, w_ref, b_ref, o_ref):
    c = c_ref[...]
    o_ref[...] = _dot_f32(c * _sigmoid(c), w_ref[...]) + b_ref[...]


def _ada_call(c_all, w_ada, b_ada):
    depth, d, n = w_ada.shape
    rows = c_all.shape[0]
    tn = 1024
    return pl.pallas_call(
        _ada_kernel,
        out_shape=jax.ShapeDtypeStruct((depth, rows, n), F32),
        grid=(depth, n // tn),
        in_specs=[
            pl.BlockSpec((rows, d), lambda l, j: (0, 0)),
            pl.BlockSpec((None, d, tn), lambda l, j: (l, 0, j)),
            pl.BlockSpec((None, 1, tn), lambda l, j: (l, 0, j)),
        ],
        out_specs=pl.BlockSpec((None, rows, tn), lambda l, j: (l, 0, j)),
        compiler_params=_cparams(("arbitrary", "arbitrary")),
        name="adaln",
    )(c_all, w_ada, b_ada.reshape(depth, 1, n))


def _mod_spec(mod, tm, sub, d):
    if mod.shape[0] == 1:
        return pl.BlockSpec((1, 3 * d), lambda i: (0, sub))
    return pl.BlockSpec((tm, 3 * d), lambda i: (i, sub))


def _ffn_body(x, mod_ref, g_ref, wup_ref, wdn_ref, fc):
    d = x.shape[-1]
    f = wdn_ref.shape[0]
    h = _rms_mod(x, g_ref[...], mod_ref[:, 0:d], mod_ref[:, d:2 * d]).astype(BF16)
    acc = jnp.zeros(x.shape, F32)
    for c in range(f // fc):
        a = _dot(h, wup_ref[:, c * fc:(c + 1) * fc])
        g = _dot(h, wup_ref[:, f + c * fc:f + (c + 1) * fc])
        act = (g * _sigmoid(g) * a).astype(BF16)
        acc = acc + _dot(act, wdn_ref[c * fc:(c + 1) * fc, :])
    return x + 0.5 * mod_ref[:, 2 * d:3 * d] * acc


def _ffn_kernel(x_ref, mod_ref, g_ref, wup_ref, wdn_ref, o_ref, *, fc):
    o_ref[...] = _ffn_body(x_ref[...], mod_ref, g_ref, wup_ref, wdn_ref, fc)


def _outproj_ffn_kernel(x_ref, ya_ref, sgb_ref, yatt_ref, wout_ref, modo_ref, mod_ref, g_ref, wup_ref, wdn_ref,
                        o_ref, *, fc):
    x = x_ref[...]
    d = x.shape[-1]
    merged = (ya_ref[...] + sgb_ref[...] * yatt_ref[...]).astype(BF16)
    x = x + modo_ref[:, 2 * d:3 * d] * _dot(merged, wout_ref[...])
    o_ref[...] = _ffn_body(x, mod_ref, g_ref, wup_ref, wdn_ref, fc)


def _ffn_chunk(f):
    for fc in (512, 256, 128):
        if f % fc == 0:
            return fc
    return f


def _ffn_call(x, mod, sub, g, wup, wdn, tm):
    t, d = x.shape
    row = pl.BlockSpec((tm, d), lambda i: (i, 0))
    return pl.pallas_call(
        functools.partial(_ffn_kernel, fc=_ffn_chunk(wdn[0].shape[-2])),
        out_shape=jax.ShapeDtypeStruct((t, d), F32),
        grid=(t // tm,),
        in_specs=[row, _mod_spec(mod, tm, sub, d), _resident((1, d)), _stacked_weight_spec(wup),
                  _stacked_weight_spec(wdn)],
        out_specs=row,
        compiler_params=_cparams(("parallel",)),
        name="ffn",
    )(x, mod, g.reshape(1, d), wup[0], wdn[0])


def _outproj_ffn_call(x, ya, sgb, yatt, wout, mod, sub_out, sub, g, wup, wdn, tm):
    t, d = x.shape
    row = pl.BlockSpec((tm, d), lambda i: (i, 0))
    return pl.pallas_call(
        functools.partial(_outproj_ffn_kernel, fc=_ffn_chunk(wdn[0].shape[-2])),
        out_shape=jax.ShapeDtypeStruct((t, d), F32),
        grid=(t // tm,),
        in_specs=[row, row, row, row, _stacked_weight_spec(wout), _mod_spec(mod, tm, sub_out, d),
                  _mod_spec(mod, tm, sub, d), _resident((1, d)), _stacked_weight_spec(wup),
                  _stacked_weight_spec(wdn)],
        out_specs=row,
        compiler_params=_cparams(("parallel",)),
        name="outproj_ffn",
    )(x, ya, sgb, yatt, wout[0], mod, mod, g.reshape(1, d), wup[0], wdn[0])


def _head_rms(u, g, nheads, hd, scale):
    outs = []
    for h in range(nheads):
        uh = u[:, h * hd:(h + 1) * hd]
        ms = jnp.mean(uh * uh, axis=-1, keepdims=True)
        outs.append(uh * lax.rsqrt(ms + EPS) * g * scale)
    return outs


def _inproj_kernel(x_ref, mod_ref, g_ref, win_ref, qg_ref, kg_ref, *out_refs, dr, nh, nkv, hd, prompt, nprev):
    out_refs = out_refs[nprev:]
    if prompt:
        xr_ref, gg_ref, sgb_ref, q_ref, k_ref, v_ref, kb_ref, vt_ref, km_ref = out_refs
    else:
        xr_ref, gg_ref, sgb_ref, q_ref, k_ref, v_ref = out_refs
    x = x_ref[...]
    tm, d = x.shape
    h = _rms_mod(x, g_ref[...], mod_ref[:, 0:d], mod_ref[:, d:2 * d]).astype(BF16)
    dq, dkv = nh * hd, nkv * hd
    o_q = 2 * dr
    o_k = o_q + dq
    o_v = o_k + dkv
    o_ga = o_v + dkv
    o_gb = o_ga + d

    def seg(lo, width):
        return _dot(h, win_ref[:, lo:lo + width])

    xr_ref[...] = seg(0, dr)
    gg_ref[...] = _sigmoid(seg(o_ga, d)) * _gelu_tanh(seg(dr, dr))
    sgb_ref[...] = _sigmoid(seg(o_gb, d))
    qs = _head_rms(seg(o_q, dq), qg_ref[...], nh, hd, hd ** -0.5)
    for hh in range(nh):
        q_ref[:, hh * hd:(hh + 1) * hd] = qs[hh]
    ks = _head_rms(seg(o_k, dkv), kg_ref[...], nkv, hd, 1.0)
    v = seg(o_v, dkv)
    for hh in range(nkv):
        k_ref[pl.ds(hh, tm, stride=nkv), :] = ks[hh]
        v_ref[pl.ds(hh, tm, stride=nkv), :] = v[:, hh * hd:(hh + 1) * hd]
    if prompt:
        vt = v.T.astype(BF16)
        hv = hd + BF16_ROWS
        for hh in range(nkv):
            vt_ref[hh * hv:hh * hv + hd, :] = vt[hh * hd:(hh + 1) * hd, :]
            vt_ref[hh * hv + hd:(hh + 1) * hv, :] = jnp.ones((BF16_ROWS, tm), BF16)
            kb_ref[:, hh * hd:(hh + 1) * hd] = ks[hh].astype(BF16)
            for b in range(tm // MOBA_BLOCK):
                km_ref[b, :, hh * hd:(hh + 1) * hd] = jnp.mean(
                    ks[hh][b * MOBA_BLOCK:(b + 1) * MOBA_BLOCK], axis=0, keepdims=True)


def _inproj_call(x, mod, g, win, qg, kg, tm, *, dr, nh, nkv, hd, prompt, slot=0, nslots=1, kv_prev=()):
    t, d = x.shape
    dq, dkv = nh * hd, nkv * hd
    nt = t // tm
    row = lambda w: pl.BlockSpec((tm, w), lambda i: (i, 0))
    out_shape = [jax.ShapeDtypeStruct((t, w), F32) for w in (dr, d, d, dq)]
    out_specs = [row(w) for w in (dr, d, d, dq)]
    out_shape += [jax.ShapeDtypeStruct((nslots * t * nkv, hd), F32)] * 2
    out_specs += [pl.BlockSpec((tm * nkv, hd), lambda i: (slot * nt + i, 0))] * 2
    n_in = 6
    if prompt:
        nbt = tm // MOBA_BLOCK
        dvt = nkv * (hd + BF16_ROWS)
        out_shape += [jax.ShapeDtypeStruct((t, dkv), BF16), jax.ShapeDtypeStruct((dvt, t), BF16),
                      jax.ShapeDtypeStruct((t // MOBA_BLOCK, 1, dkv), F32)]
        out_specs += [row(dkv), pl.BlockSpec((dvt, tm), lambda i: (0, i)),
                      pl.BlockSpec((nbt, 1, dkv), lambda i: (i, 0, 0))]
    return pl.pallas_call(
        functools.partial(_inproj_kernel, dr=dr, nh=nh, nkv=nkv, hd=hd, prompt=prompt, nprev=len(kv_prev)),
        out_shape=out_shape,
        grid=(nt,),
        in_specs=[row(d), _mod_spec(mod, tm, 1, d), _resident((1, d)), _stacked_weight_spec(win),
                  _resident((1, hd)), _resident((1, hd))] + [pl.BlockSpec(memory_space=pl.ANY)] * len(kv_prev),
        out_specs=out_specs,
        input_output_aliases={n_in + j: 4 + j for j in range(len(kv_prev))},
        compiler_params=_cparams(("parallel",)),
        name="inproj",
    )(x, mod, g.reshape(1, d), win[0], qg.reshape(1, hd), kg.reshape(1, hd), *kv_prev)


def _rglru_gates(xc, wg_ref, ba_ref, bx_ref, lam_ref, a_scr, b_scr):
    ngroups, gw, _ = wg_ref.shape
    xcb = xc.astype(BF16)
    for gi in range(ngroups):
        cols = slice(gi * gw, (gi + 1) * gw)
        u = _dot(xcb[:, cols], wg_ref[gi])
        r = _sigmoid(u[:, :gw] + ba_ref[:, cols])
        gate_x = _sigmoid(u[:, gw:] + bx_ref[:, cols])
        lam = lam_ref[:, cols]
        softplus_neg = jnp.maximum(-lam, 0.0) + jnp.log(1.0 + jnp.exp(-jnp.abs(lam)))
        log_a = -LRU_C * r * softplus_neg
        a = jnp.exp(log_a)
        mult = jnp.sqrt(-_expm1(2.0 * log_a))
        a_scr[:, cols] = a
        b_scr[:, cols] = mult * gate_x * xc[:, cols]


def _expm1(x):
    small = x * (1.0 + x * (0.5 + x * (1.0 / 6.0 + x * (1.0 / 24.0 + x * (1.0 / 120.0)))))
    return jnp.where(x > -0.1, small, jnp.exp(x) - 1.0)


def _scan8(a, b):
    row = lax.broadcasted_iota(jnp.int32, a.shape, 0)
    for k in (1, 2, 4):
        keep = row >= k
        a_sh = jnp.where(keep, pltpu.roll(a, k, 0), 1.0)
        b_sh = jnp.where(keep, pltpu.roll(b, k, 0), 0.0)
        b = a * b_sh + b
        a = a * a_sh
    return a, b


def _rglru_prompt_kernel(xr_ref, gg_ref, cw_ref, cb_ref, wg_ref, ba_ref, bx_ref, lam_ref, ya_ref, hlast_ref,
                         xbuf, a_scr, b_scr, hcar):
    i = pl.program_id(0)
    tm, dr = xr_ref.shape
    cwid = cw_ref.shape[0]

    @pl.when(i == 0)
    def _():
        xbuf[0:SUBLANES, :] = jnp.zeros((SUBLANES, dr), F32)
        hcar[...] = jnp.zeros_like(hcar)

    xbuf[SUBLANES:SUBLANES + tm, :] = xr_ref[...]
    xc = cb_ref[...] + jnp.zeros((tm, dr), F32)
    for j in range(cwid):
        off = SUBLANES - (cwid - 1) + j
        xc = xc + cw_ref[j:j + 1, :] * xbuf[off:off + tm, :]
    xbuf[0:SUBLANES, :] = xbuf[tm:tm + SUBLANES, :]
    _rglru_gates(xc, wg_ref, ba_ref, bx_ref, lam_ref, a_scr, b_scr)

    def body(g, carry):
        rows = pl.ds(pl.multiple_of(g * SUBLANES, SUBLANES), SUBLANES)
        a, b = _scan8(a_scr[rows, :], b_scr[rows, :])
        h = a * carry + b
        ya_ref[rows, :] = h * gg_ref[rows, :]
        return h[SUBLANES - 1:SUBLANES, :]

    hc = lax.fori_loop(0, tm // SUBLANES, body, hcar[...])
    hcar[...] = hc
    hlast_ref[...] = hc


def _rglru_prompt_call(xr, gg, cw, cb, wg, ba, bx, lam, tm):
    t, dr = xr.shape
    row = pl.BlockSpec((tm, dr), lambda i: (i, 0))
    vec = _resident((1, dr))
    return pl.pallas_call(
        _rglru_prompt_kernel,
        out_shape=[jax.ShapeDtypeStruct((t, dr), F32), jax.ShapeDtypeStruct((1, dr), F32)],
        grid=(t // tm,),
        in_specs=[row, row, _resident(cw.shape), vec, _resident(wg.shape), vec, vec, vec],
        out_specs=[row, pl.BlockSpec((1, dr), lambda i: (0, 0))],
        scratch_shapes=[pltpu.VMEM((tm + SUBLANES, dr), F32), pltpu.VMEM((tm, dr), F32),
                        pltpu.VMEM((tm, dr), F32), pltpu.VMEM((1, dr), F32)],
        compiler_params=_cparams(("arbitrary",)),
        name="rglru_prompt",
    )(xr, gg, cw, cb.reshape(1, dr), wg, ba.reshape(1, dr), bx.reshape(1, dr), lam.reshape(1, dr))


def _rglru_sample_kernel(xcat_ref, gg_ref, h0_ref, cw_ref, cb_ref, wg_ref, ba_ref, bx_ref, lam_ref, ya_ref, h_ref,
                         xc_scr, a_scr, b_scr):
    nseq, _, dr = xcat_ref.shape
    cwid = cw_ref.shape[0]

    def conv(s, _):
        xc = cb_ref[...] + jnp.zeros((SUBLANES, dr), F32)
        for j in range(cwid):
            xc = xc + cw_ref[j:j + 1, :] * xcat_ref[s, j:j + SUBLANES, :]
        xc_scr[pl.ds(pl.multiple_of(s * SUBLANES, SUBLANES), SUBLANES), :] = xc
        return 0

    lax.fori_loop(0, nseq, conv, 0)
    _rglru_gates(xc_scr[...], wg_ref, ba_ref, bx_ref, lam_ref, a_scr, b_scr)

    def body(s, _):
        rows = pl.ds(pl.multiple_of(s * SUBLANES, SUBLANES), SUBLANES)
        a, b = _scan8(a_scr[rows, :], b_scr[rows, :])
        h = a * h0_ref[s] + b
        h_ref[rows, :] = h
        ya_ref[rows, :] = h * gg_ref[rows, :]
        return 0

    lax.fori_loop(0, nseq, body, 0)


def _rglru_sample_call(xcat, gg, h0, cw, cb, wg, ba, bx, lam):
    nseq, _, dr = xcat.shape
    t = nseq * SUBLANES
    full = lambda shape: pl.BlockSpec(shape, lambda i: (0,) * len(shape))
    return pl.pallas_call(
        _rglru_sample_kernel,
        out_shape=[jax.ShapeDtypeStruct((t, dr), F32), jax.ShapeDtypeStruct((t, dr), F32)],
        grid=(1,),
        in_specs=[full(xcat.shape), full((t, dr)), full((nseq, 1, dr)), full(cw.shape), full((1, dr)),
                  full(wg.shape), full((1, dr)), full((1, dr)), full((1, dr))],
        out_specs=[full((t, dr)), full((t, dr))],
        scratch_shapes=[pltpu.VMEM((t, dr), F32), pltpu.VMEM((t, dr), F32), pltpu.VMEM((t, dr), F32)],
        compiler_params=_cparams(("arbitrary",)),
        name="rglru_sample",
    )(xcat, gg, h0.reshape(nseq, 1, dr), cw, cb.reshape(1, dr), wg, ba.reshape(1, dr), bx.reshape(1, dr),
      lam.reshape(1, dr))


def _select_blocks(gs, n_eligible, axis):
    nb = gs.shape[axis]
    jj = lax.broadcasted_iota(jnp.int32, gs.shape, axis)
    s = jnp.where(jj < n_eligible, gs, -jnp.inf)
    sel = jnp.zeros(gs.shape, F32)
    for p in range(MOBA_TOP_K):
        m = jnp.max(s, axis=axis, keepdims=True)
        idx = jnp.min(jnp.where(s == m, jj, nb), axis=axis, keepdims=True)
        hit = jj == idx
        sel = jnp.where(jnp.logical_and(hit, jj * 0 + p < n_eligible), 1.0, sel)
        s = jnp.where(hit, -jnp.inf, s)
    return sel


def _attn_prompt_kernel(q_ref, qn_ref, k_ref, vt_ref, km_ref, near_ref, e_ref, place_ref, o_ref,
                        qa_scr, fm_scr, fmn_scr, adjn_scr, m_scr, acc_scr, s0_scr, s1_scr):
    i = pl.program_id(1)
    tq, hd2 = q_ref.shape
    hd = hd2 // 2
    blk = MOBA_BLOCK
    cb = FAR_BLOCKS * blk
    nb = km_ref.shape[0]
    nchunk = nb // FAR_BLOCKS
    nslab = (nchunk + 1) * BF16_ROWS
    prow = lax.broadcasted_iota(jnp.int32, (nslab, 2 * tq), 0)
    chunk_masked = jnp.where(prow % BF16_ROWS < FAR_BLOCKS, NEG, 0.0)

    @pl.when(i == 0)
    def _():
        fmn_scr[...] = chunk_masked.astype(BF16)
        adjn_scr[...] = jnp.zeros((1, 2 * tq), F32)

    fm_scr[...] = fmn_scr[...]
    adj_sel = adjn_scr[...]

    def transposed(qr):
        q2 = qr[...]
        return jnp.concatenate([q2[:, :hd].T, q2[:, hd:].T], axis=1)

    qa_scr[0:hd, :] = (transposed(q_ref) * LOG2E).astype(BF16)
    qa_scr[hd + BF16_ROWS:2 * hd, :] = jnp.zeros((hd - BF16_ROWS, 2 * tq), BF16)
    qtb = qa_scr[0:hd, :]

    ja = pl.multiple_of(jnp.maximum(i - 1, 0) * blk, blk)
    jo = pl.multiple_of(i * blk, blk)
    s_adj = _dot(k_ref[pl.ds(ja, blk), :], qtb) + (near_ref[0] + jnp.where(adj_sel > 0.5, 0.0, NEG))
    s_own = _dot(k_ref[pl.ds(jo, blk), :], qtb) + near_ref[1]

    nch = jnp.maximum(i + FAR_BLOCKS - 2, 0) // FAR_BLOCKS

    def chunk_offset(c):
        return pl.multiple_of(jnp.minimum(c, nchunk - 1) * cb, cb)

    def scores(c, s_ref):
        slab = pl.multiple_of(jnp.where(c < nch, c, nchunk) * BF16_ROWS, BF16_ROWS)
        qa_scr[hd:hd + BF16_ROWS, :] = fm_scr[pl.ds(slab, BF16_ROWS), :]
        kc = jnp.concatenate([k_ref[pl.ds(chunk_offset(c), cb), :], e_ref[...]], axis=1)
        st = _dot(kc, qa_scr[...])
        s_ref[0:cb, :] = st
        s_ref[cb:cb + 1, :] = jnp.max(st, axis=0, keepdims=True)

    def accumulate(c, s_ref):
        m_old = m_scr[...]
        m_new = jnp.maximum(m_old, s_ref[cb:cb + 1, :])
        p = jnp.exp2(s_ref[0:cb, :] - m_new).astype(BF16)
        acc_scr[...] = jnp.exp2(m_old - m_new) * acc_scr[...] + _dot(vt_ref[:, pl.ds(chunk_offset(c), cb)], p)
        m_scr[...] = m_new

    scores(0, s0_scr)
    gate = _dot_f32(km_ref[...], transposed(qn_ref))
    m = jnp.maximum(jnp.max(s_adj, axis=0, keepdims=True), jnp.max(s_own, axis=0, keepdims=True))
    p_adj = jnp.exp2(s_adj - m).astype(BF16)
    p_own = jnp.exp2(s_own - m).astype(BF16)
    sel = _select_blocks(gate, i + 1, 0)
    acc_scr[...] = _dot(vt_ref[:, pl.ds(ja, blk)], p_adj) + _dot(vt_ref[:, pl.ds(jo, blk)], p_own)
    m_scr[...] = m
    jj = lax.broadcasted_iota(jnp.int32, sel.shape, 0)
    adjn_scr[...] = jnp.max(jnp.where(jj == i, sel, 0.0), axis=0, keepdims=True)
    placed = _dot(place_ref[...], jnp.where(jj < i, sel, 0.0).astype(BF16))
    placed = jnp.concatenate([placed, jnp.zeros((BF16_ROWS, 2 * tq), F32)], axis=0)
    fmn_scr[...] = jnp.where(placed > 0.5, 0.0, chunk_masked).astype(BF16)

    def far_body(t, _):
        scores(2 * t + 1, s1_scr)
        accumulate(2 * t, s0_scr)
        scores(2 * t + 2, s0_scr)
        accumulate(2 * t + 1, s1_scr)
        return 0

    lax.fori_loop(0, nch // 2, far_body, 0)

    @pl.when(nch % 2 == 1)
    def _():
        accumulate(nch - 1, s0_scr)

    acc = acc_scr[...]
    ot = acc[:hd] / acc[hd:hd + 1]
    o_ref[:, :hd] = ot[:, :tq].T
    o_ref[:, hd:] = ot[:, tq:].T


def _attn_prompt_call(q, kb, vt, km, near, nkv, hd):
    s = q.shape[0]
    tq = MOBA_BLOCK
    nb = s // tq
    assert nb % FAR_BLOCKS == 0 and nb <= hd
    cb = FAR_BLOCKS * tq
    nchunk = nb // FAR_BLOCKS
    hv = hd + BF16_ROWS
    e = (jnp.arange(hd)[None, :] == (jnp.arange(cb) // tq)[:, None]).astype(BF16)
    pr = jnp.arange(nchunk * BF16_ROWS)
    place = ((pr % BF16_ROWS < FAR_BLOCKS)[:, None]
             & ((pr // BF16_ROWS * FAR_BLOCKS + pr % BF16_ROWS)[:, None] == jnp.arange(nb)[None, :])).astype(BF16)
    return pl.pallas_call(
        _attn_prompt_kernel,
        out_shape=jax.ShapeDtypeStruct(q.shape, F32),
        grid=(nkv, nb),
        in_specs=[
            pl.BlockSpec((tq, 2 * hd), lambda h, i: (i, h)),
            pl.BlockSpec((tq, 2 * hd), lambda h, i: (jnp.minimum(i + 1, nb - 1), h)),
            pl.BlockSpec((s, hd), lambda h, i: (0, h)),
            pl.BlockSpec((hv, s), lambda h, i: (h, 0)),
            pl.BlockSpec((nb, hd), lambda h, i: (0, h)),
            pl.BlockSpec((None, 2, tq, 2 * tq), lambda h, i: (h, 0, 0, 0)),
            _resident(e.shape),
            _resident(place.shape),
        ],
        out_specs=pl.BlockSpec((tq, 2 * hd), lambda h, i: (i, h)),
        scratch_shapes=[pltpu.VMEM((2 * hd, 2 * tq), BF16), pltpu.VMEM(((nchunk + 1) * BF16_ROWS, 2 * tq), BF16),
                        pltpu.VMEM(((nchunk + 1) * BF16_ROWS, 2 * tq), BF16),
                        pltpu.VMEM((1, 2 * tq), F32), pltpu.VMEM((1, 2 * tq), F32), pltpu.VMEM((hv, 2 * tq), F32),
                        pltpu.VMEM((cb + SUBLANES, 2 * tq), F32), pltpu.VMEM((cb + SUBLANES, 2 * tq), F32)],
        compiler_params=_cparams(("arbitrary", "arbitrary")),
        name="moba_prompt",
    )(q, q, kb, vt, km, near, e, place)


def _paged(layer, p, npg):
    def index(b, g, pt):
        return (layer, pt[b, g * npg + p], 0, 0)
    return index


def _pv_ones(p, v, ones):
    return _dot(p.astype(BF16), jnp.concatenate([v.astype(BF16), ones[:v.shape[0]]], axis=1))


def _sample_blocks_kernel(pt_ref, q_ref, base_ref, last_ref, *refs, nkv, ppb):
    npg = (len(refs) - 2) // 2
    kpages, vpages = refs[:npg], refs[npg:2 * npg]
    part_ref, km_ref = refs[2 * npg:]
    is_last = pl.program_id(1) == pl.num_programs(1) - 1
    rows, hd = q_ref.shape
    prow = kpages[0].shape[0]
    qb = (q_ref[...] * LOG2E).astype(BF16)
    ones = jnp.ones((prow, hd), BF16)
    lane = lax.broadcasted_iota(jnp.int32, (rows, 2 * hd), 1)
    bps = npg // ppb
    for b in range(bps):
        tot = jnp.zeros((SUBLANES, hd), F32)
        ss = []
        for r in range(ppb):
            kp = kpages[b * ppb + r][...]
            tot = tot + jnp.sum(kp.reshape(prow // SUBLANES, SUBLANES, hd), axis=0)
            tile = base_ref[...]
            if b == bps - 1:
                tile = jnp.where(is_last, last_ref[r], tile)
            ss.append(_dot_nt(qb, kp.astype(BF16)) + tile)
        tot = tot + pltpu.roll(tot, nkv, 0)
        km_ref[b] = tot * (1.0 / (ppb * (prow // nkv)))
        mx = ss[0]
        for s in ss[1:]:
            mx = jnp.maximum(mx, s)
        m = jnp.max(mx, axis=1, keepdims=True)
        part = _pv_ones(jnp.exp2(ss[0] - m), vpages[b * ppb][...], ones)
        for r in range(1, ppb):
            part = part + _pv_ones(jnp.exp2(ss[r] - m), vpages[b * ppb + r][...], ones)
        part_ref[b] = jnp.where(lane == 2 * hd - 1, m, part)


def _sample_blocks_call(qall, base, last, ck, cv, page_table, layer, nb, nkv):
    _, _, prow, hd = ck.shape
    bs, npages = page_table.shape
    ppb = npages // nb
    npg = PAGES_PER_STEP
    assert 2 * nkv == SUBLANES and npg % ppb == 0 and npages % npg == 0
    bps = npg // ppb
    rows = qall.shape[1]
    page_spec = lambda p: pl.BlockSpec((None, None, prow, hd), _paged(layer, p, npg))
    const = lambda shape: pl.BlockSpec(shape, lambda b, g, pt: (0,) * len(shape))
    grid_spec = pltpu.PrefetchScalarGridSpec(
        num_scalar_prefetch=1,
        grid=(bs, npages // npg),
        in_specs=[pl.BlockSpec((None, rows, hd), lambda b, g, pt: (b, 0, 0)), const(base.shape), const(last.shape)]
        + [page_spec(p) for p in range(npg)] * 2,
        out_specs=[pl.BlockSpec((None, bps, rows, 2 * hd), lambda b, g, pt: (b, g, 0, 0)),
                   pl.BlockSpec((None, bps, SUBLANES, hd), lambda b, g, pt: (b, g, 0, 0))],
    )
    return pl.pallas_call(
        functools.partial(_sample_blocks_kernel, nkv=nkv, ppb=ppb),
        out_shape=[jax.ShapeDtypeStruct((bs, nb, rows, 2 * hd), F32),
                   jax.ShapeDtypeStruct((bs, nb, SUBLANES, hd), F32)],
        grid_spec=grid_spec,
        compiler_params=_cparams(("parallel", "parallel")),
        name="sample_blocks",
    )(page_table, qall, base, last, *([ck] * npg), *([cv] * npg))


def _sample_merge_kernel(q_ref, kn_ref, vn_ref, km_ref, own_ref, part_ref, o_ref, *, nkv):
    rows, hd = q_ref.shape
    rph = rows // nkv
    nb = km_ref.shape[1]
    q = q_ref[...]
    qb = (q * LOG2E).astype(BF16)
    sel = jnp.concatenate(
        [_select_blocks(_dot_nt_f32(q[h * rph:(h + 1) * rph], km_ref[h]), nb, 1) for h in range(nkv)], axis=0)
    pad = jnp.zeros((LANES - kn_ref.shape[0], hd), F32)
    s = _dot_nt(qb, jnp.concatenate([kn_ref[...], pad], axis=0).astype(BF16)) + own_ref[...]
    m0 = jnp.max(s, axis=1, keepdims=True)
    acc0 = _pv_ones(jnp.exp2(s - m0), jnp.concatenate([vn_ref[...], pad], axis=0), jnp.ones((LANES, hd), BF16))
    lane = lax.broadcasted_iota(jnp.int32, (rows, nb), 1)
    mb = jnp.zeros((rows, nb), F32)
    for j in range(nb):
        mb = jnp.where(lane == j, part_ref[j, :, 2 * hd - 1:2 * hd], mb)
    picked = sel > 0.5
    m = jnp.maximum(m0, jnp.max(jnp.where(picked, mb, NEG), axis=1, keepdims=True))
    w = jnp.where(picked, jnp.exp2(mb - m), 0.0)
    acc = jnp.exp2(m0 - m) * acc0
    for j in range(nb):
        acc = acc + w[:, j:j + 1] * part_ref[j]
    o_ref[...] = acc[:, :hd] / acc[:, hd:hd + 1]


def _sample_merge_call(qall, kn, vn, km, own, parts, nkv):
    bs, rows, hd = qall.shape
    nb = km.shape[2]
    nown = kn.shape[1]
    assert nown <= LANES
    per_seq = lambda *shape: pl.BlockSpec((None,) + shape, lambda b: (b,) + (0,) * len(shape))
    return pl.pallas_call(
        functools.partial(_sample_merge_kernel, nkv=nkv),
        out_shape=jax.ShapeDtypeStruct(qall.shape, F32),
        grid=(bs,),
        in_specs=[per_seq(rows, hd), per_seq(nown, hd), per_seq(nown, hd), per_seq(nkv, nb, hd),
                  pl.BlockSpec(own.shape, lambda b: (0, 0)), per_seq(nb, rows, 2 * hd)],
        out_specs=per_seq(rows, hd),
        compiler_params=_cparams(("parallel",)),
        name="sample_merge",
    )(qall, kn, vn, km, own, parts)


def _rel_bucket(dist, n_buckets):
    n = jnp.maximum(dist, 0)
    max_exact = n_buckets // 2
    nf = jnp.maximum(n, 1).astype(F32)
    large = max_exact + (jnp.log(nf / max_exact) / math.log(MAX_DISTANCE / max_exact)
                         * (n_buckets - max_exact)).astype(jnp.int32)
    large = jnp.minimum(large, n_buckets - 1)
    return jnp.where(n < max_exact, n, large)


def _shifted_bias(dist, rel_table):
    nbk = rel_table.shape[0]
    far = rel_table[_rel_bucket(jnp.asarray(2 * MOBA_BLOCK), nbk)]
    b = (rel_table[_rel_bucket(dist, nbk)] - far) * LOG2E
    return jnp.where((dist >= 0)[..., None], b, NEG)


def _skew_toeplitz(u, n):
    lead = u.shape[:-1]
    up = jnp.concatenate([u, jnp.zeros(lead + (1,), u.dtype)], axis=-1)
    flat = jnp.broadcast_to(up[..., None, :], lead + (n, 2 * n)).reshape(lead + (2 * n * n,))
    return flat[..., :n * (2 * n - 1)].reshape(lead + (n, 2 * n - 1))


def _prompt_near_tiles(rel_table, nkv):
    blk = MOBA_BLOCK
    x = jnp.arange(2 * blk - 1) - (blk - 1)
    u = jnp.stack([_shifted_bias(x + blk, rel_table).T, _shifted_bias(x, rel_table).T])
    a = _skew_toeplitz(u, blk)[..., blk - 1:2 * blk - 1]
    a = a.reshape(2, nkv, 2, blk, blk)
    return jnp.transpose(a, (1, 0, 3, 2, 4)).reshape(nkv, 2, blk, 2 * blk)


def _sample_tiles(rel_table, nkv, lq, psz, ppb):
    rows = 2 * nkv * lq
    row_kvh = jnp.arange(rows) // (2 * lq)
    row_t = jnp.arange(rows) % lq
    col_h = jnp.arange(psz * nkv) % nkv
    base = jnp.where(row_kvh[:, None] == col_h[None, :], 0.0, NEG).astype(F32)
    kk = jnp.arange(ppb * psz)
    b = _shifted_bias(jnp.arange(lq)[:, None] + ppb * psz - kk[None, :], rel_table)
    b = jnp.transpose(b, (2, 0, 1)).reshape(rows, ppb, psz)
    last = jnp.transpose(b, (1, 0, 2))[:, :, :, None] + base.reshape(1, rows, psz, nkv)
    last = last.reshape(ppb, rows, psz * nkv)
    ncol = LANES
    col_t, col_hh = jnp.arange(ncol) // nkv, jnp.arange(ncol) % nkv
    ob = _shifted_bias(jnp.arange(lq)[:, None] - col_t[None, :], rel_table)
    ob = jnp.transpose(ob, (2, 0, 1)).reshape(rows, ncol)
    ok = (row_kvh[:, None] == col_hh[None, :]) & (col_t[None, :] < lq)
    own = jnp.where(ok, ob, NEG)
    return base, last, own


def _block_diag_gates(w_a, w_x, group):
    n, c, _ = w_a.shape
    eye = jnp.eye(group, dtype=w_a.dtype)

    def bd(w):
        w = w.reshape(n // group, group, c, c)
        return jnp.einsum('gbcd,be->gbced', w, eye).reshape(n // group, group * c, group * c)

    return jnp.concatenate([bd(w_a), bd(w_x)], axis=-1).astype(BF16)


def kernel(x_prompt, x_sample, c_prompt, c_sample, cache_k, cache_v, page_table, state_conv, state_rglru, norm_g,
           w_ada, b_ada, w_in, conv_w, conv_b, w_rg_a, b_rg_a, w_rg_x, b_rg_x, lru_lambda, q_norm_g, k_norm_g,
           rel_table, w_out, w_ffn_in, w_ffn_out):
    depth = norm_g.shape[0]
    bp, s, d = x_prompt.shape
    bs, lq, _ = x_sample.shape
    _, npool, psz, nkv, hd = cache_k.shape
    nh = rel_table.shape[1]
    dr = conv_w.shape[2]
    cwid = conv_w.shape[1]
    past_len = page_table.shape[1] * psz
    nb_past = past_len // MOBA_BLOCK
    assert bp == 1 and s % MOBA_BLOCK == 0 and lq == SUBLANES and nh == 2 * nkv
    assert past_len == nb_past * MOBA_BLOCK and MOBA_BLOCK % psz == 0 and cwid - 1 <= lq
    assert MAX_DISTANCE <= MOBA_BLOCK
    dkv = nkv * hd
    ppb = MOBA_BLOCK // psz
    tm_p = 512 if s % 512 == 0 else MOBA_BLOCK
    ts = bs * lq

    c_all = jnp.concatenate([c_prompt, c_sample], axis=0)
    nrow = -(-c_all.shape[0] // SUBLANES) * SUBLANES
    c_all = jnp.pad(c_all, ((0, nrow - c_all.shape[0]), (0, 0)))
    mods = _ada_call(c_all, w_ada, b_ada)

    ck = cache_k.reshape(depth, npool, psz * nkv, hd)
    cv = cache_v.reshape(depth, npool, psz * nkv, hd)
    near_p = _prompt_near_tiles(rel_table, nkv)
    base_s, last_s, own_s = _sample_tiles(rel_table, nkv, lq, psz, ppb)

    wup_all, wdn_all = w_ffn_in.astype(BF16), w_ffn_out.astype(BF16)
    win_all, wout_all = w_in.astype(BF16), w_out.astype(BF16)

    xp = x_prompt.reshape(s, d)
    xs = x_sample.reshape(ts, d)
    outs = {k: [] for k in ('cp', 'hp', 'ks', 'vs', 'cs', 'hs')}
    kv_p = ()
    for l in range(depth):
        mod_p = mods[l, 0:bp]
        mod_s = jnp.repeat(mods[l, bp:bp + bs], lq, axis=0)
        wup0, wup1 = (wup_all, (l, 0)), (wup_all, (l, 1))
        wdn0, wdn1 = (wdn_all, (l, 0)), (wdn_all, (l, 1))
        win, wout = (win_all, (l,)), (wout_all, (l,))
        wg = _block_diag_gates(w_rg_a[l], w_rg_x[l], 4)
        rg = (conv_w[l], conv_b[l], wg, b_rg_a[l], b_rg_x[l], lru_lambda[l])
        proj = dict(dr=dr, nh=nh, nkv=nkv, hd=hd)

        xp = _ffn_call(xp, mod_p, 0, norm_g[l, 0], wup0, wdn0, tm_p)
        xr, gg, sgb, q, *kv_p, kb, vt, km = _inproj_call(xp, mod_p, norm_g[l, 1], win, q_norm_g[l], k_norm_g[l], tm_p,
                                                         prompt=True, slot=l, nslots=depth, kv_prev=kv_p, **proj)
        ya, hlast = _rglru_prompt_call(xr, gg, *rg, tm_p)
        yatt = _attn_prompt_call(q, kb, vt, km.reshape(s // MOBA_BLOCK, dkv), near_p, nkv, hd)
        xp = _outproj_ffn_call(xp, ya, sgb, yatt, wout, mod_p, 1, 2, norm_g[l, 2], wup1, wdn1, tm_p)
        outs['cp'].append(xr[s - (cwid - 1):].reshape(bp, cwid - 1, dr))
        outs['hp'].append(hlast)

        xs = _ffn_call(xs, mod_s, 0, norm_g[l, 0], wup0, wdn0, ts)
        xr, gg, sgb, q, k, v = _inproj_call(xs, mod_s, norm_g[l, 1], win, q_norm_g[l], k_norm_g[l], ts,
                                            prompt=False, **proj)
        xr3 = xr.reshape(bs, lq, dr)
        xcat = jnp.concatenate([state_conv[l], xr3, jnp.zeros((bs, 2 * SUBLANES - lq - (cwid - 1), dr), F32)], axis=1)
        ya, h_all = _rglru_sample_call(xcat, gg, state_rglru[l], *rg)
        qall = jnp.transpose(q.reshape(bs, lq, nkv, 2, hd), (0, 2, 3, 1, 4)).reshape(bs, 2 * nkv * lq, hd)
        parts, km_s = _sample_blocks_call(qall, base_s, last_s, ck, cv, page_table, l, nb_past, nkv)
        km_s = jnp.transpose(km_s[:, :, :nkv], (0, 2, 1, 3))
        yatt = _sample_merge_call(qall, k.reshape(bs, lq * nkv, hd), v.reshape(bs, lq * nkv, hd), km_s,
                                  own_s, parts, nkv)
        yatt = jnp.transpose(yatt.reshape(bs, nkv, 2, lq, hd), (0, 3, 1, 2, 4)).reshape(ts, nh * hd)
        xs = _outproj_ffn_call(xs, ya, sgb, yatt, wout, mod_s, 1, 2, norm_g[l, 2], wup1, wdn1, ts)
        outs['ks'].append(k.reshape(bs, lq, nkv, hd))
        outs['vs'].append(v.reshape(bs, lq, nkv, hd))
        outs['cs'].append(xr3[:, lq - (cwid - 1):])
        outs['hs'].append(h_all.reshape(bs, lq, dr)[:, lq - 1])

    st = {k: jnp.stack(v) for k, v in outs.items()}
    k_p, v_p = (a.reshape(depth, bp, s, nkv, hd) for a in kv_p)
    return (xp.reshape(bp, s, d), xs.reshape(bs, lq, d), k_p, v_p, st['cp'], st['hp'],
            st['ks'], st['vs'], st['cs'], st['hs'])
```

```python
import functools
import math

import jax
import jax.numpy as jnp
from jax import lax
from jax.experimental import pallas as pl
from jax.experimental.pallas import tpu as pltpu

F32 = jnp.float32
BF16 = jnp.bfloat16

EPS = 1e-6
LRU_C = 8.0
MOBA_BLOCK = 256
MOBA_TOP_K = 3
MAX_DISTANCE = 128
NEG = -1e30
LOG2E = 1.4426950408889634

V7X_VMEM_BYTES = 64 * 1024 * 1024
VMEM_LIMIT = V7X_VMEM_BYTES * 3 // 4
SUBLANES = 8
LANES = 128
BF16_ROWS = 16
PAGES_PER_STEP = 16
TOKEN_TILE = 512
ADA_TILE_N = 1024
FAR_BLOCKS = 4


def _cparams(sem):
    return pltpu.CompilerParams(dimension_semantics=sem, vmem_limit_bytes=VMEM_LIMIT)


def _resident(shape):
    nd = len(shape)
    return pl.BlockSpec(shape, lambda *_: (0,) * nd, pipeline_mode=pl.Buffered(1))


def _stacked_weight_spec(w):
    arr, lead = w
    tail = arr.shape[len(lead):]
    return pl.BlockSpec((None,) * len(lead) + tail, lambda *_: lead + (0,) * len(tail),
                        pipeline_mode=pl.Buffered(1))


def _sigmoid(x):
    return 1.0 / (1.0 + jnp.exp(-x))


def _gelu_tanh(x):
    return 0.5 * x * (1.0 + jnp.tanh(math.sqrt(2.0 / math.pi) * (x + 0.044715 * (x * x * x))))


def _rms_mod(x, g, shift, scale):
    ms = jnp.mean(x * x, axis=-1, keepdims=True)
    y = x * lax.rsqrt(ms + EPS) * g
    return y * (1.0 + scale) + shift


def _dot(a, b):
    return jnp.dot(a, b, preferred_element_type=F32)


def _dot_nt(a, b):
    return lax.dot_general(a, b, (((1,), (1,)), ((), ())), preferred_element_type=F32)


def _dot_f32(a, b):
    return jnp.dot(a, b, preferred_element_type=F32, precision=lax.Precision.HIGHEST)


def _dot_bf16x3(a, b):
    ah, bh = a.astype(BF16), b.astype(BF16)
    al, bl = (a - ah.astype(F32)).astype(BF16), (b - bh.astype(F32)).astype(BF16)
    return _dot(ah, bh) + (_dot(ah, bl) + _dot(al, bh))


def _dot_nt_f32(a, b):
    return lax.dot_general(a, b, (((1,), (1,)), ((), ())), preferred_element_type=F32,
                           precision=lax.Precision.HIGHEST)


def _ada_kernel(c_ref, w_ref, b_ref, o_ref):
    c = c_ref[...]
    o_ref[...] = _dot_f32(c * _sigmoid(c), w_ref[...]) + b_ref[...]


def _ada_call(c_all, w_ada, b_ada):
    depth, d, n = w_ada.shape
    rows = c_all.shape[0]
    tn = ADA_TILE_N
    return pl.pallas_call(
        _ada_kernel,
        out_shape=jax.ShapeDtypeStruct((depth, rows, n), F32),
        grid=(depth, n // tn),
        in_specs=[
            pl.BlockSpec((rows, d), lambda l, j: (0, 0)),
            pl.BlockSpec((None, d, tn), lambda l, j: (l, 0, j)),
            pl.BlockSpec((None, 1, tn), lambda l, j: (l, 0, j)),
        ],
        out_specs=pl.BlockSpec((None, rows, tn), lambda l, j: (l, 0, j)),
        compiler_params=_cparams(("arbitrary", "arbitrary")),
        name="adaln",
    )(c_all, w_ada, b_ada.reshape(depth, 1, n))


def _mod_spec(mod, tm, sub, d):
    if mod.shape[0] == 1:
        return pl.BlockSpec((1, 3 * d), lambda i: (0, sub))
    return pl.BlockSpec((tm, 3 * d), lambda i: (i, sub))


def _ffn_body(x, mod_ref, g_ref, wup_ref, wdn_ref, fc):
    d = x.shape[-1]
    f = wdn_ref.shape[0]
    h = _rms_mod(x, g_ref[...], mod_ref[:, 0:d], mod_ref[:, d:2 * d]).astype(BF16)
    acc = jnp.zeros(x.shape, F32)
    for c in range(f // fc):
        a = _dot(h, wup_ref[:, c * fc:(c + 1) * fc])
        g = _dot(h, wup_ref[:, f + c * fc:f + (c + 1) * fc])
        act = (g * _sigmoid(g) * a).astype(BF16)
        acc = acc + _dot(act, wdn_ref[c * fc:(c + 1) * fc, :])
    return x + 0.5 * mod_ref[:, 2 * d:3 * d] * acc


def _ffn_kernel(x_ref, mod_ref, g_ref, wup_ref, wdn_ref, o_ref, *, fc):
    o_ref[...] = _ffn_body(x_ref[...], mod_ref, g_ref, wup_ref, wdn_ref, fc)


def _outproj_ffn_kernel(x_ref, ya_ref, sgb_ref, yatt_ref, wout_ref, modo_ref, mod_ref, g_ref, wup_ref, wdn_ref,
                        o_ref, *, fc):
    x = x_ref[...]
    d = x.shape[-1]
    merged = (ya_ref[...] + sgb_ref[...] * yatt_ref[...]).astype(BF16)
    x = x + modo_ref[:, 2 * d:3 * d] * _dot(merged, wout_ref[...])
    o_ref[...] = _ffn_body(x, mod_ref, g_ref, wup_ref, wdn_ref, fc)


def _ffn_chunk(f):
    for fc in (512, 256, 128):
        if f % fc == 0:
            return fc
    return f


def _ffn_call(x, mod, sub, g, wup, wdn, tm):
    t, d = x.shape
    row = pl.BlockSpec((tm, d), lambda i: (i, 0))
    return pl.pallas_call(
        functools.partial(_ffn_kernel, fc=_ffn_chunk(wdn[0].shape[-2])),
        out_shape=jax.ShapeDtypeStruct((t, d), F32),
        grid=(t // tm,),
        in_specs=[row, _mod_spec(mod, tm, sub, d), _resident((1, d)), _stacked_weight_spec(wup),
                  _stacked_weight_spec(wdn)],
        out_specs=row,
        compiler_params=_cparams(("parallel",)),
        name="ffn",
    )(x, mod, g.reshape(1, d), wup[0], wdn[0])


def _outproj_ffn_call(x, ya, sgb, yatt, wout, mod, sub_out, sub, g, wup, wdn, tm):
    t, d = x.shape
    row = pl.BlockSpec((tm, d), lambda i: (i, 0))
    return pl.pallas_call(
        functools.partial(_outproj_ffn_kernel, fc=_ffn_chunk(wdn[0].shape[-2])),
        out_shape=jax.ShapeDtypeStruct((t, d), F32),
        grid=(t // tm,),
        in_specs=[row, row, row, row, _stacked_weight_spec(wout), _mod_spec(mod, tm, sub_out, d),
                  _mod_spec(mod, tm, sub, d), _resident((1, d)), _stacked_weight_spec(wup),
                  _stacked_weight_spec(wdn)],
        out_specs=row,
        compiler_params=_cparams(("parallel",)),
        name="outproj_ffn",
    )(x, ya, sgb, yatt, wout[0], mod, mod, g.reshape(1, d), wup[0], wdn[0])


def _head_rms(u, g, nheads, hd, scale):
    outs = []
    for h in range(nheads):
        uh = u[:, h * hd:(h + 1) * hd]
        ms = jnp.mean(uh * uh, axis=-1, keepdims=True)
        outs.append(uh * lax.rsqrt(ms + EPS) * g * scale)
    return outs


def _inproj_kernel(x_ref, mod_ref, g_ref, win_ref, qg_ref, kg_ref, *out_refs, dr, nh, nkv, hd, prompt, nprev):
    out_refs = out_refs[nprev:]
    if prompt:
        xr_ref, gg_ref, sgb_ref, q_ref, k_ref, v_ref, kb_ref, vt_ref, km_ref = out_refs
    else:
        xr_ref, gg_ref, sgb_ref, q_ref, k_ref, v_ref = out_refs
    x = x_ref[...]
    tm, d = x.shape
    h = _rms_mod(x, g_ref[...], mod_ref[:, 0:d], mod_ref[:, d:2 * d]).astype(BF16)
    dq, dkv = nh * hd, nkv * hd
    o_q = 2 * dr
    o_k = o_q + dq
    o_v = o_k + dkv
    o_ga = o_v + dkv
    o_gb = o_ga + d

    def seg(lo, width):
        return _dot(h, win_ref[:, lo:lo + width])

    xr_ref[...] = seg(0, dr)
    gg_ref[...] = _sigmoid(seg(o_ga, d)) * _gelu_tanh(seg(dr, dr))
    sgb_ref[...] = _sigmoid(seg(o_gb, d))
    qs = _head_rms(seg(o_q, dq), qg_ref[...], nh, hd, hd ** -0.5)
    for hh in range(nh):
        q_ref[:, hh * hd:(hh + 1) * hd] = qs[hh]
    ks = _head_rms(seg(o_k, dkv), kg_ref[...], nkv, hd, 1.0)
    v = seg(o_v, dkv)
    for hh in range(nkv):
        k_ref[pl.ds(hh, tm, stride=nkv), :] = ks[hh]
        v_ref[pl.ds(hh, tm, stride=nkv), :] = v[:, hh * hd:(hh + 1) * hd]
    if prompt:
        vt = v.T.astype(BF16)
        hv = hd + BF16_ROWS
        for hh in range(nkv):
            vt_ref[hh * hv:hh * hv + hd, :] = vt[hh * hd:(hh + 1) * hd, :]
            vt_ref[hh * hv + hd:(hh + 1) * hv, :] = jnp.ones((BF16_ROWS, tm), BF16)
            kb_ref[:, hh * hd:(hh + 1) * hd] = ks[hh].astype(BF16)
            for b in range(tm // MOBA_BLOCK):
                km_ref[b, :, hh * hd:(hh + 1) * hd] = jnp.mean(
                    ks[hh][b * MOBA_BLOCK:(b + 1) * MOBA_BLOCK], axis=0, keepdims=True)


def _inproj_call(x, mod, g, win, qg, kg, tm, *, dr, nh, nkv, hd, prompt, slot=0, nslots=1, kv_prev=()):
    t, d = x.shape
    dq, dkv = nh * hd, nkv * hd
    nt = t // tm
    row = lambda w: pl.BlockSpec((tm, w), lambda i: (i, 0))
    out_shape = [jax.ShapeDtypeStruct((t, w), F32) for w in (dr, d, d, dq)]
    out_specs = [row(w) for w in (dr, d, d, dq)]
    out_shape += [jax.ShapeDtypeStruct((nslots * t * nkv, hd), F32)] * 2
    out_specs += [pl.BlockSpec((tm * nkv, hd), lambda i: (slot * nt + i, 0))] * 2
    n_in = 6
    if prompt:
        nbt = tm // MOBA_BLOCK
        dvt = nkv * (hd + BF16_ROWS)
        out_shape += [jax.ShapeDtypeStruct((t, dkv), BF16), jax.ShapeDtypeStruct((dvt, t), BF16),
                      jax.ShapeDtypeStruct((t // MOBA_BLOCK, 1, dkv), F32)]
        out_specs += [row(dkv), pl.BlockSpec((dvt, tm), lambda i: (0, i)),
                      pl.BlockSpec((nbt, 1, dkv), lambda i: (i, 0, 0))]
    return pl.pallas_call(
        functools.partial(_inproj_kernel, dr=dr, nh=nh, nkv=nkv, hd=hd, prompt=prompt, nprev=len(kv_prev)),
        out_shape=out_shape,
        grid=(nt,),
        in_specs=[row(d), _mod_spec(mod, tm, 1, d), _resident((1, d)), _stacked_weight_spec(win),
                  _resident((1, hd)), _resident((1, hd))] + [pl.BlockSpec(memory_space=pl.ANY)] * len(kv_prev),
        out_specs=out_specs,
        input_output_aliases={n_in + j: 4 + j for j in range(len(kv_prev))},
        compiler_params=_cparams(("parallel",)),
        name="inproj",
    )(x, mod, g.reshape(1, d), win[0], qg.reshape(1, hd), kg.reshape(1, hd), *kv_prev)


def _rglru_gates(xc, wg_ref, ba_ref, bx_ref, lam_ref, a_scr, b_scr):
    ngroups, gw, _ = wg_ref.shape
    xcb = xc.astype(BF16)
    for gi in range(ngroups):
        cols = slice(gi * gw, (gi + 1) * gw)
        u = _dot(xcb[:, cols], wg_ref[gi])
        r = _sigmoid(u[:, :gw] + ba_ref[:, cols])
        gate_x = _sigmoid(u[:, gw:] + bx_ref[:, cols])
        lam = lam_ref[:, cols]
        softplus_neg = jnp.maximum(-lam, 0.0) + jnp.log(1.0 + jnp.exp(-jnp.abs(lam)))
        log_a = -LRU_C * r * softplus_neg
        a = jnp.exp(log_a)
        mult = jnp.sqrt(-_expm1(2.0 * log_a))
        a_scr[:, cols] = a
        b_scr[:, cols] = mult * gate_x * xc[:, cols]


def _expm1(x):
    small = x * (1.0 + x * (0.5 + x * (1.0 / 6.0 + x * (1.0 / 24.0 + x * (1.0 / 120.0)))))
    return jnp.where(x > -0.1, small, jnp.exp(x) - 1.0)


def _scan8(a, b):
    row = lax.broadcasted_iota(jnp.int32, a.shape, 0)
    for k in (1, 2, 4):
        keep = row >= k
        a_sh = jnp.where(keep, pltpu.roll(a, k, 0), 1.0)
        b_sh = jnp.where(keep, pltpu.roll(b, k, 0), 0.0)
        b = a * b_sh + b
        a = a * a_sh
    return a, b


def _rglru_prompt_kernel(xr_ref, gg_ref, cw_ref, cb_ref, wg_ref, ba_ref, bx_ref, lam_ref, ya_ref, hlast_ref,
                         xbuf, a_scr, b_scr, hcar):
    i = pl.program_id(0)
    tm, dr = xr_ref.shape
    cwid = cw_ref.shape[0]

    @pl.when(i == 0)
    def _():
        xbuf[0:SUBLANES, :] = jnp.zeros((SUBLANES, dr), F32)
        hcar[...] = jnp.zeros_like(hcar)

    xbuf[SUBLANES:SUBLANES + tm, :] = xr_ref[...]
    xc = cb_ref[...] + jnp.zeros((tm, dr), F32)
    for j in range(cwid):
        off = SUBLANES - (cwid - 1) + j
        xc = xc + cw_ref[j:j + 1, :] * xbuf[off:off + tm, :]
    xbuf[0:SUBLANES, :] = xbuf[tm:tm + SUBLANES, :]
    _rglru_gates(xc, wg_ref, ba_ref, bx_ref, lam_ref, a_scr, b_scr)

    def body(g, carry):
        rows = pl.ds(pl.multiple_of(g * SUBLANES, SUBLANES), SUBLANES)
        a, b = _scan8(a_scr[rows, :], b_scr[rows, :])
        h = a * carry + b
        ya_ref[rows, :] = h * gg_ref[rows, :]
        return h[SUBLANES - 1:SUBLANES, :]

    hc = lax.fori_loop(0, tm // SUBLANES, body, hcar[...])
    hcar[...] = hc
    hlast_ref[...] = hc


def _rglru_prompt_call(xr, gg, cw, cb, wg, ba, bx, lam, tm):
    t, dr = xr.shape
    row = pl.BlockSpec((tm, dr), lambda i: (i, 0))
    vec = _resident((1, dr))
    return pl.pallas_call(
        _rglru_prompt_kernel,
        out_shape=[jax.ShapeDtypeStruct((t, dr), F32), jax.ShapeDtypeStruct((1, dr), F32)],
        grid=(t // tm,),
        in_specs=[row, row, _resident(cw.shape), vec, _resident(wg.shape), vec, vec, vec],
        out_specs=[row, pl.BlockSpec((1, dr), lambda i: (0, 0))],
        scratch_shapes=[pltpu.VMEM((tm + SUBLANES, dr), F32), pltpu.VMEM((tm, dr), F32),
                        pltpu.VMEM((tm, dr), F32), pltpu.VMEM((1, dr), F32)],
        compiler_params=_cparams(("arbitrary",)),
        name="rglru_prompt",
    )(xr, gg, cw, cb.reshape(1, dr), wg, ba.reshape(1, dr), bx.reshape(1, dr), lam.reshape(1, dr))


def _rglru_sample_kernel(xcat_ref, gg_ref, h0_ref, cw_ref, cb_ref, wg_ref, ba_ref, bx_ref, lam_ref, ya_ref, h_ref,
                         xc_scr, a_scr, b_scr):
    nseq, _, dr = xcat_ref.shape
    cwid = cw_ref.shape[0]

    def conv(s, _):
        xc = cb_ref[...] + jnp.zeros((SUBLANES, dr), F32)
        for j in range(cwid):
            xc = xc + cw_ref[j:j + 1, :] * xcat_ref[s, j:j + SUBLANES, :]
        xc_scr[pl.ds(pl.multiple_of(s * SUBLANES, SUBLANES), SUBLANES), :] = xc
        return 0

    lax.fori_loop(0, nseq, conv, 0)
    _rglru_gates(xc_scr[...], wg_ref, ba_ref, bx_ref, lam_ref, a_scr, b_scr)

    def body(s, _):
        rows = pl.ds(pl.multiple_of(s * SUBLANES, SUBLANES), SUBLANES)
        a, b = _scan8(a_scr[rows, :], b_scr[rows, :])
        h = a * h0_ref[s] + b
        h_ref[rows, :] = h
        ya_ref[rows, :] = h * gg_ref[rows, :]
        return 0

    lax.fori_loop(0, nseq, body, 0)


def _rglru_sample_call(xcat, gg, h0, cw, cb, wg, ba, bx, lam):
    nseq, _, dr = xcat.shape
    t = nseq * SUBLANES
    full = lambda shape: pl.BlockSpec(shape, lambda i: (0,) * len(shape))
    return pl.pallas_call(
        _rglru_sample_kernel,
        out_shape=[jax.ShapeDtypeStruct((t, dr), F32), jax.ShapeDtypeStruct((t, dr), F32)],
        grid=(1,),
        in_specs=[full(xcat.shape), full((t, dr)), full((nseq, 1, dr)), full(cw.shape), full((1, dr)),
                  full(wg.shape), full((1, dr)), full((1, dr)), full((1, dr))],
        out_specs=[full((t, dr)), full((t, dr))],
        scratch_shapes=[pltpu.VMEM((t, dr), F32), pltpu.VMEM((t, dr), F32), pltpu.VMEM((t, dr), F32)],
        compiler_params=_cparams(("arbitrary",)),
        name="rglru_sample",
    )(xcat, gg, h0.reshape(nseq, 1, dr), cw, cb.reshape(1, dr), wg, ba.reshape(1, dr), bx.reshape(1, dr),
      lam.reshape(1, dr))


def _select_blocks(gs, n_eligible, axis):
    nb = gs.shape[axis]
    jj = lax.broadcasted_iota(jnp.int32, gs.shape, axis)
    s = jnp.where(jj < n_eligible, gs, -jnp.inf)
    sel = jnp.zeros(gs.shape, F32)
    for p in range(MOBA_TOP_K):
        m = jnp.max(s, axis=axis, keepdims=True)
        idx = jnp.min(jnp.where(s == m, jj, nb), axis=axis, keepdims=True)
        hit = jj == idx
        sel = jnp.where(jnp.logical_and(hit, jj * 0 + p < n_eligible), 1.0, sel)
        s = jnp.where(hit, -jnp.inf, s)
    return sel


def _attn_prompt_kernel(q_ref, qn_ref, k_ref, vt_ref, km_ref, near_ref, e_ref, place_ref, o_ref,
                        qa_scr, fm_scr, fmn_scr, adjn_scr, m_scr, acc_scr, s0_scr, s1_scr):
    i = pl.program_id(1)
    tq, hd2 = q_ref.shape
    hd = hd2 // 2
    blk = MOBA_BLOCK
    cb = FAR_BLOCKS * blk
    nb = km_ref.shape[0]
    nchunk = nb // FAR_BLOCKS
    nslab = (nchunk + 1) * BF16_ROWS
    prow = lax.broadcasted_iota(jnp.int32, (nslab, 2 * tq), 0)
    chunk_masked = jnp.where(prow % BF16_ROWS < FAR_BLOCKS, NEG, 0.0)

    @pl.when(i == 0)
    def _():
        fmn_scr[...] = chunk_masked.astype(BF16)
        adjn_scr[...] = jnp.zeros((1, 2 * tq), F32)

    fm_scr[...] = fmn_scr[...]
    adj_sel = adjn_scr[...]

    def transposed(qr):
        q2 = qr[...]
        return jnp.concatenate([q2[:, :hd].T, q2[:, hd:].T], axis=1)

    qa_scr[0:hd, :] = (transposed(q_ref) * LOG2E).astype(BF16)
    qa_scr[hd + BF16_ROWS:2 * hd, :] = jnp.zeros((hd - BF16_ROWS, 2 * tq), BF16)
    qtb = qa_scr[0:hd, :]

    gate = _dot_bf16x3(km_ref[...], transposed(qn_ref))

    jn = pl.multiple_of(jnp.maximum(i - 1, 0) * blk, blk)
    first = i == 0
    tile_lo = jnp.where(first, near_ref[1], near_ref[0] + jnp.where(adj_sel > 0.5, 0.0, NEG))
    tile_hi = jnp.where(first, NEG, near_ref[1])
    s_near = _dot(k_ref[pl.ds(jn, 2 * blk), :], qtb) + jnp.concatenate([tile_lo, tile_hi], axis=0)

    nch = jnp.maximum(i + FAR_BLOCKS - 2, 0) // FAR_BLOCKS

    def chunk_offset(c):
        return pl.multiple_of(jnp.minimum(c, nchunk - 1) * cb, cb)

    def scores(c, s_ref):
        slab = pl.multiple_of(jnp.where(c < nch, c, nchunk) * BF16_ROWS, BF16_ROWS)
        qa_scr[hd:hd + BF16_ROWS, :] = fm_scr[pl.ds(slab, BF16_ROWS), :]
        kc = jnp.concatenate([k_ref[pl.ds(chunk_offset(c), cb), :], e_ref[...]], axis=1)
        st = _dot(kc, qa_scr[...])
        s_ref[0:cb, :] = st
        s_ref[cb:cb + 1, :] = jnp.max(st, axis=0, keepdims=True)

    def accumulate(c, s_ref):
        m_old = m_scr[...]
        m_new = jnp.maximum(m_old, s_ref[cb:cb + 1, :])
        p = jnp.exp2(s_ref[0:cb, :] - m_new).astype(BF16)
        acc_scr[...] = jnp.exp2(m_old - m_new) * acc_scr[...] + _dot(vt_ref[:, pl.ds(chunk_offset(c), cb)], p)
        m_scr[...] = m_new

    scores(0, s0_scr)
    m = jnp.max(s_near, axis=0, keepdims=True)
    p_near = jnp.exp2(s_near - m).astype(BF16)
    sel = _select_blocks(gate, i + 1, 0)
    acc_scr[...] = _dot(vt_ref[:, pl.ds(jn, 2 * blk)], p_near)
    m_scr[...] = m
    jj = lax.broadcasted_iota(jnp.int32, sel.shape, 0)
    adjn_scr[...] = jnp.max(jnp.where(jj == i, sel, 0.0), axis=0, keepdims=True)
    placed = _dot(place_ref[...], jnp.where(jj < i, sel, 0.0).astype(BF16))
    placed = jnp.concatenate([placed, jnp.zeros((BF16_ROWS, 2 * tq), F32)], axis=0)
    fmn_scr[...] = jnp.where(placed > 0.5, 0.0, chunk_masked).astype(BF16)

    def far_body(t, _):
        scores(2 * t + 1, s1_scr)
        accumulate(2 * t, s0_scr)
        scores(2 * t + 2, s0_scr)
        accumulate(2 * t + 1, s1_scr)
        return 0

    lax.fori_loop(0, jnp.maximum(nch - 1, 0) // 2, far_body, 0)

    @pl.when(nch % 2 == 1)
    def _():
        accumulate(nch - 1, s0_scr)

    @pl.when(jnp.logical_and(nch % 2 == 0, nch > 0))
    def _():
        scores(nch - 1, s1_scr)
        accumulate(nch - 2, s0_scr)
        accumulate(nch - 1, s1_scr)

    acc = acc_scr[...]
    ot = acc[:hd] / acc[hd:hd + 1]
    o_ref[:, :hd] = ot[:, :tq].T
    o_ref[:, hd:] = ot[:, tq:].T


def _attn_prompt_call(q, kb, vt, km, near, nkv, hd):
    s = q.shape[0]
    tq = MOBA_BLOCK
    nb = s // tq
    assert nb % FAR_BLOCKS == 0 and nb <= hd
    cb = FAR_BLOCKS * tq
    nchunk = nb // FAR_BLOCKS
    hv = hd + BF16_ROWS
    e = (jnp.arange(hd)[None, :] == (jnp.arange(cb) // tq)[:, None]).astype(BF16)
    pr = jnp.arange(nchunk * BF16_ROWS)
    place = ((pr % BF16_ROWS < FAR_BLOCKS)[:, None]
             & ((pr // BF16_ROWS * FAR_BLOCKS + pr % BF16_ROWS)[:, None] == jnp.arange(nb)[None, :])).astype(BF16)
    return pl.pallas_call(
        _attn_prompt_kernel,
        out_shape=jax.ShapeDtypeStruct(q.shape, F32),
        grid=(nkv, nb),
        in_specs=[
            pl.BlockSpec((tq, 2 * hd), lambda h, i: (i, h)),
            pl.BlockSpec((tq, 2 * hd), lambda h, i: (jnp.minimum(i + 1, nb - 1), h)),
            pl.BlockSpec((s, hd), lambda h, i: (0, h)),
            pl.BlockSpec((hv, s), lambda h, i: (h, 0)),
            pl.BlockSpec((nb, hd), lambda h, i: (0, h)),
            pl.BlockSpec((None, 2, tq, 2 * tq), lambda h, i: (h, 0, 0, 0)),
            _resident(e.shape),
            _resident(place.shape),
        ],
        out_specs=pl.BlockSpec((tq, 2 * hd), lambda h, i: (i, h)),
        scratch_shapes=[pltpu.VMEM((2 * hd, 2 * tq), BF16), pltpu.VMEM(((nchunk + 1) * BF16_ROWS, 2 * tq), BF16),
                        pltpu.VMEM(((nchunk + 1) * BF16_ROWS, 2 * tq), BF16),
                        pltpu.VMEM((1, 2 * tq), F32), pltpu.VMEM((1, 2 * tq), F32), pltpu.VMEM((hv, 2 * tq), F32),
                        pltpu.VMEM((cb + SUBLANES, 2 * tq), F32), pltpu.VMEM((cb + SUBLANES, 2 * tq), F32)],
        compiler_params=_cparams(("arbitrary", "arbitrary")),
        name="moba_prompt",
    )(q, q, kb, vt, km, near, e, place)


def _paged(layer, p, npg):
    def index(b, g, pt):
        return (layer, pt[b, g * npg + p], 0, 0)
    return index


def _pv_ones(p, v, ones):
    return _dot(p.astype(BF16), jnp.concatenate([v.astype(BF16), ones[:v.shape[0]]], axis=1))


def _sample_blocks_kernel(pt_ref, q_ref, base_ref, last_ref, *refs, nkv, ppb):
    npg = (len(refs) - 2) // 2
    kpages, vpages = refs[:npg], refs[npg:2 * npg]
    part_ref, km_ref = refs[2 * npg:]
    is_last = pl.program_id(1) == pl.num_programs(1) - 1
    rows, hd = q_ref.shape
    prow = kpages[0].shape[0]
    qb = (q_ref[...] * LOG2E).astype(BF16)
    ones = jnp.ones((prow, hd), BF16)
    lane = lax.broadcasted_iota(jnp.int32, (rows, 2 * hd), 1)
    bps = npg // ppb
    for b in range(bps):
        tot = jnp.zeros((SUBLANES, hd), F32)
        ss = []
        for r in range(ppb):
            kp = kpages[b * ppb + r][...]
            tot = tot + jnp.sum(kp.reshape(prow // SUBLANES, SUBLANES, hd), axis=0)
            tile = base_ref[...]
            if b == bps - 1:
                tile = jnp.where(is_last, last_ref[r], tile)
            ss.append(_dot_nt(qb, kp.astype(BF16)) + tile)
        tot = tot + pltpu.roll(tot, nkv, 0)
        km_ref[b] = tot * (1.0 / (ppb * (prow // nkv)))
        mx = ss[0]
        for s in ss[1:]:
            mx = jnp.maximum(mx, s)
        m = jnp.max(mx, axis=1, keepdims=True)
        part = _pv_ones(jnp.exp2(ss[0] - m), vpages[b * ppb][...], ones)
        for r in range(1, ppb):
            part = part + _pv_ones(jnp.exp2(ss[r] - m), vpages[b * ppb + r][...], ones)
        part_ref[b] = jnp.where(lane == 2 * hd - 1, m, part)


def _sample_blocks_call(qall, base, last, ck, cv, page_table, layer, nb, nkv):
    _, _, prow, hd = ck.shape
    bs, npages = page_table.shape
    ppb = npages // nb
    npg = PAGES_PER_STEP
    assert 2 * nkv == SUBLANES and npg % ppb == 0 and npages % npg == 0
    bps = npg // ppb
    rows = qall.shape[1]
    page_spec = lambda p: pl.BlockSpec((None, None, prow, hd), _paged(layer, p, npg))
    const = lambda shape: pl.BlockSpec(shape, lambda b, g, pt: (0,) * len(shape))
    grid_spec = pltpu.PrefetchScalarGridSpec(
        num_scalar_prefetch=1,
        grid=(bs, npages // npg),
        in_specs=[pl.BlockSpec((None, rows, hd), lambda b, g, pt: (b, 0, 0)), const(base.shape), const(last.shape)]
        + [page_spec(p) for p in range(npg)] * 2,
        out_specs=[pl.BlockSpec((None, bps, rows, 2 * hd), lambda b, g, pt: (b, g, 0, 0)),
                   pl.BlockSpec((None, bps, SUBLANES, hd), lambda b, g, pt: (b, g, 0, 0))],
    )
    return pl.pallas_call(
        functools.partial(_sample_blocks_kernel, nkv=nkv, ppb=ppb),
        out_shape=[jax.ShapeDtypeStruct((bs, nb, rows, 2 * hd), F32),
                   jax.ShapeDtypeStruct((bs, nb, SUBLANES, hd), F32)],
        grid_spec=grid_spec,
        compiler_params=_cparams(("parallel", "parallel")),
        name="sample_blocks",
    )(page_table, qall, base, last, *([ck] * npg), *([cv] * npg))


def _sample_merge_kernel(q_ref, kn_ref, vn_ref, km_ref, own_ref, part_ref, o_ref, *, nkv):
    rows, hd = q_ref.shape
    rph = rows // nkv
    nb = km_ref.shape[1]
    q = q_ref[...]
    qb = (q * LOG2E).astype(BF16)
    sel = jnp.concatenate(
        [_select_blocks(_dot_nt_f32(q[h * rph:(h + 1) * rph], km_ref[h]), nb, 1) for h in range(nkv)], axis=0)
    pad = jnp.zeros((LANES - kn_ref.shape[0], hd), F32)
    s = _dot_nt(qb, jnp.concatenate([kn_ref[...], pad], axis=0).astype(BF16)) + own_ref[...]
    m0 = jnp.max(s, axis=1, keepdims=True)
    acc0 = _pv_ones(jnp.exp2(s - m0), jnp.concatenate([vn_ref[...], pad], axis=0), jnp.ones((LANES, hd), BF16))
    lane = lax.broadcasted_iota(jnp.int32, (rows, nb), 1)
    mb = jnp.zeros((rows, nb), F32)
    for j in range(nb):
        mb = jnp.where(lane == j, part_ref[j, :, 2 * hd - 1:2 * hd], mb)
    picked = sel > 0.5
    m = jnp.maximum(m0, jnp.max(jnp.where(picked, mb, NEG), axis=1, keepdims=True))
    w = jnp.where(picked, jnp.exp2(mb - m), 0.0)
    acc = jnp.exp2(m0 - m) * acc0
    for j in range(nb):
        acc = acc + w[:, j:j + 1] * part_ref[j]
    o_ref[...] = acc[:, :hd] / acc[:, hd:hd + 1]


def _sample_merge_call(qall, kn, vn, km, own, parts, nkv):
    bs, rows, hd = qall.shape
    nb = km.shape[2]
    nown = kn.shape[1]
    assert nown <= LANES
    per_seq = lambda *shape: pl.BlockSpec((None,) + shape, lambda b: (b,) + (0,) * len(shape))
    return pl.pallas_call(
        functools.partial(_sample_merge_kernel, nkv=nkv),
        out_shape=jax.ShapeDtypeStruct(qall.shape, F32),
        grid=(bs,),
        in_specs=[per_seq(rows, hd), per_seq(nown, hd), per_seq(nown, hd), per_seq(nkv, nb, hd),
                  pl.BlockSpec(own.shape, lambda b: (0, 0)), per_seq(nb, rows, 2 * hd)],
        out_specs=per_seq(rows, hd),
        compiler_params=_cparams(("parallel",)),
        name="sample_merge",
    )(qall, kn, vn, km, own, parts)


def _rel_bucket(dist, n_buckets):
    n = jnp.maximum(dist, 0)
    max_exact = n_buckets // 2
    nf = jnp.maximum(n, 1).astype(F32)
    large = max_exact + (jnp.log(nf / max_exact) / math.log(MAX_DISTANCE / max_exact)
                         * (n_buckets - max_exact)).astype(jnp.int32)
    large = jnp.minimum(large, n_buckets - 1)
    return jnp.where(n < max_exact, n, large)


def _shifted_bias(dist, rel_table):
    nbk = rel_table.shape[0]
    far = rel_table[_rel_bucket(jnp.asarray(2 * MOBA_BLOCK), nbk)]
    b = (rel_table[_rel_bucket(dist, nbk)] - far) * LOG2E
    return jnp.where((dist >= 0)[..., None], b, NEG)


def _skew_toeplitz(u, n):
    lead = u.shape[:-1]
    up = jnp.concatenate([u, jnp.zeros(lead + (1,), u.dtype)], axis=-1)
    flat = jnp.broadcast_to(up[..., None, :], lead + (n, 2 * n)).reshape(lead + (2 * n * n,))
    return flat[..., :n * (2 * n - 1)].reshape(lead + (n, 2 * n - 1))


def _prompt_near_tiles(rel_table, nkv):
    blk = MOBA_BLOCK
    x = jnp.arange(2 * blk - 1) - (blk - 1)
    u = jnp.stack([_shifted_bias(x + blk, rel_table).T, _shifted_bias(x, rel_table).T])
    a = _skew_toeplitz(u, blk)[..., blk - 1:2 * blk - 1]
    a = a.reshape(2, nkv, 2, blk, blk)
    return jnp.transpose(a, (1, 0, 3, 2, 4)).reshape(nkv, 2, blk, 2 * blk)


def _sample_tiles(rel_table, nkv, lq, psz, ppb):
    rows = 2 * nkv * lq
    row_kvh = jnp.arange(rows) // (2 * lq)
    row_t = jnp.arange(rows) % lq
    col_h = jnp.arange(psz * nkv) % nkv
    base = jnp.where(row_kvh[:, None] == col_h[None, :], 0.0, NEG).astype(F32)
    kk = jnp.arange(ppb * psz)
    b = _shifted_bias(jnp.arange(lq)[:, None] + ppb * psz - kk[None, :], rel_table)
    b = jnp.transpose(b, (2, 0, 1)).reshape(rows, ppb, psz)
    last = jnp.transpose(b, (1, 0, 2))[:, :, :, None] + base.reshape(1, rows, psz, nkv)
    last = last.reshape(ppb, rows, psz * nkv)
    ncol = LANES
    col_t, col_hh = jnp.arange(ncol) // nkv, jnp.arange(ncol) % nkv
    ob = _shifted_bias(jnp.arange(lq)[:, None] - col_t[None, :], rel_table)
    ob = jnp.transpose(ob, (2, 0, 1)).reshape(rows, ncol)
    ok = (row_kvh[:, None] == col_hh[None, :]) & (col_t[None, :] < lq)
    own = jnp.where(ok, ob, NEG)
    return base, last, own


def _block_diag_gates(w_a, w_x, group):
    n, c, _ = w_a.shape
    eye = jnp.eye(group, dtype=w_a.dtype)

    def bd(w):
        w = w.reshape(n // group, group, c, c)
        return jnp.einsum('gbcd,be->gbced', w, eye).reshape(n // group, group * c, group * c)

    return jnp.concatenate([bd(w_a), bd(w_x)], axis=-1).astype(BF16)


def kernel(x_prompt, x_sample, c_prompt, c_sample, cache_k, cache_v, page_table, state_conv, state_rglru, norm_g,
           w_ada, b_ada, w_in, conv_w, conv_b, w_rg_a, b_rg_a, w_rg_x, b_rg_x, lru_lambda, q_norm_g, k_norm_g,
           rel_table, w_out, w_ffn_in, w_ffn_out):
    depth = norm_g.shape[0]
    bp, s, d = x_prompt.shape
    bs, lq, _ = x_sample.shape
    _, npool, psz, nkv, hd = cache_k.shape
    nh = rel_table.shape[1]
    dr = conv_w.shape[2]
    cwid = conv_w.shape[1]
    past_len = page_table.shape[1] * psz
    nb_past = past_len // MOBA_BLOCK
    assert bp == 1 and s % MOBA_BLOCK == 0 and lq == SUBLANES and nh == 2 * nkv
    assert past_len == nb_past * MOBA_BLOCK and MOBA_BLOCK % psz == 0 and cwid - 1 <= lq
    assert MAX_DISTANCE <= MOBA_BLOCK
    dkv = nkv * hd
    ppb = MOBA_BLOCK // psz
    tm_p = TOKEN_TILE if s % TOKEN_TILE == 0 else MOBA_BLOCK
    ts = bs * lq

    c_all = jnp.concatenate([c_prompt, c_sample], axis=0)
    nrow = -(-c_all.shape[0] // SUBLANES) * SUBLANES
    c_all = jnp.pad(c_all, ((0, nrow - c_all.shape[0]), (0, 0)))
    mods = _ada_call(c_all, w_ada, b_ada)

    ck = cache_k.reshape(depth, npool, psz * nkv, hd)
    cv = cache_v.reshape(depth, npool, psz * nkv, hd)
    near_p = _prompt_near_tiles(rel_table, nkv)
    base_s, last_s, own_s = _sample_tiles(rel_table, nkv, lq, psz, ppb)

    wup_all, wdn_all = w_ffn_in.astype(BF16), w_ffn_out.astype(BF16)
    win_all, wout_all = w_in.astype(BF16), w_out.astype(BF16)

    xp = x_prompt.reshape(s, d)
    xs = x_sample.reshape(ts, d)
    outs = {k: [] for k in ('cp', 'hp', 'ks', 'vs', 'cs', 'hs')}
    kv_p = ()
    for l in range(depth):
        mod_p = mods[l, 0:bp]
        mod_s = jnp.repeat(mods[l, bp:bp + bs], lq, axis=0)
        wup0, wup1 = (wup_all, (l, 0)), (wup_all, (l, 1))
        wdn0, wdn1 = (wdn_all, (l, 0)), (wdn_all, (l, 1))
        win, wout = (win_all, (l,)), (wout_all, (l,))
        wg = _block_diag_gates(w_rg_a[l], w_rg_x[l], 4)
        rg = (conv_w[l], conv_b[l], wg, b_rg_a[l], b_rg_x[l], lru_lambda[l])
        proj = dict(dr=dr, nh=nh, nkv=nkv, hd=hd)

        xp = _ffn_call(xp, mod_p, 0, norm_g[l, 0], wup0, wdn0, tm_p)
        xr, gg, sgb, q, *kv_p, kb, vt, km = _inproj_call(xp, mod_p, norm_g[l, 1], win, q_norm_g[l], k_norm_g[l], tm_p,
                                                         prompt=True, slot=l, nslots=depth, kv_prev=kv_p, **proj)
        ya, hlast = _rglru_prompt_call(xr, gg, *rg, tm_p)
        yatt = _attn_prompt_call(q, kb, vt, km.reshape(s // MOBA_BLOCK, dkv), near_p, nkv, hd)
        xp = _outproj_ffn_call(xp, ya, sgb, yatt, wout, mod_p, 1, 2, norm_g[l, 2], wup1, wdn1, tm_p)
        outs['cp'].append(xr[s - (cwid - 1):].reshape(bp, cwid - 1, dr))
        outs['hp'].append(hlast)

        xs = _ffn_call(xs, mod_s, 0, norm_g[l, 0], wup0, wdn0, ts)
        xr, gg, sgb, q, k, v = _inproj_call(xs, mod_s, norm_g[l, 1], win, q_norm_g[l], k_norm_g[l], ts,
                                            prompt=False, **proj)
        xr3 = xr.reshape(bs, lq, dr)
        xcat = jnp.concatenate([state_conv[l], xr3, jnp.zeros((bs, 2 * SUBLANES - lq - (cwid - 1), dr), F32)], axis=1)
        ya, h_all = _rglru_sample_call(xcat, gg, state_rglru[l], *rg)
        qall = jnp.transpose(q.reshape(bs, lq, nkv, 2, hd), (0, 2, 3, 1, 4)).reshape(bs, 2 * nkv * lq, hd)
        parts, km_s = _sample_blocks_call(qall, base_s, last_s, ck, cv, page_table, l, nb_past, nkv)
        km_s = jnp.transpose(km_s[:, :, :nkv], (0, 2, 1, 3))
        yatt = _sample_merge_call(qall, k.reshape(bs, lq * nkv, hd), v.reshape(bs, lq * nkv, hd), km_s,
                                  own_s, parts, nkv)
        yatt = jnp.transpose(yatt.reshape(bs, nkv, 2, lq, hd), (0, 3, 1, 2, 4)).reshape(ts, nh * hd)
        xs = _outproj_ffn_call(xs, ya, sgb, yatt, wout, mod_s, 1, 2, norm_g[l, 2], wup1, wdn1, ts)
        outs['ks'].append(k.reshape(bs, lq, nkv, hd))
        outs['vs'].append(v.reshape(bs, lq, nkv, hd))
        outs['cs'].append(xr3[:, lq - (cwid - 1):])
        outs['hs'].append(h_all.reshape(bs, lq, dr)[:, lq - 1])

    st = {k: jnp.stack(v) for k, v in outs.items()}
    k_p, v_p = (a.reshape(depth, bp, s, nkv, hd) for a in kv_p)
    return (xp.reshape(bp, s, d), xs.reshape(bs, lq, d), k_p, v_p, st['cp'], st['hp'],
            st['ks'], st['vs'], st['cs'], st['hs'])
```

```python
import functools
import math

import jax
import jax.numpy as jnp
from jax import lax
from jax.experimental import pallas as pl
from jax.experimental.pallas import tpu as pltpu

F32 = jnp.float32
BF16 = jnp.bfloat16

EPS = 1e-6
LRU_C = 8.0
MOBA_BLOCK = 256
MOBA_TOP_K = 3
MAX_DISTANCE = 128
NEG = -1e30
LOG2E = 1.4426950408889634

V7X_VMEM_BYTES = 64 * 1024 * 1024
VMEM_LIMIT = V7X_VMEM_BYTES * 3 // 4
SUBLANES = 8
LANES = 128
BF16_ROWS = 16
PAGES_PER_STEP = 16
TOKEN_TILE = 512
ADA_TILE_N = 1024
FAR_BLOCKS = 4


def _cparams(sem):
    return pltpu.CompilerParams(dimension_semantics=sem, vmem_limit_bytes=VMEM_LIMIT)


def _resident(shape):
    nd = len(shape)
    return pl.BlockSpec(shape, lambda *_: (0,) * nd, pipeline_mode=pl.Buffered(1))


def _stacked_weight_spec(w):
    arr, lead = w
    tail = arr.shape[len(lead):]
    return pl.BlockSpec((None,) * len(lead) + tail, lambda *_: lead + (0,) * len(tail),
                        pipeline_mode=pl.Buffered(1))


def _sigmoid(x):
    return 1.0 / (1.0 + jnp.exp(-x))


def _gelu_tanh(x):
    return 0.5 * x * (1.0 + jnp.tanh(math.sqrt(2.0 / math.pi) * (x + 0.044715 * (x * x * x))))


def _rms_mod(x, g, shift, scale):
    ms = jnp.mean(x * x, axis=-1, keepdims=True)
    y = x * lax.rsqrt(ms + EPS) * g
    return y * (1.0 + scale) + shift


def _dot(a, b):
    return jnp.dot(a, b, preferred_element_type=F32)


def _dot_nt(a, b):
    return lax.dot_general(a, b, (((1,), (1,)), ((), ())), preferred_element_type=F32)


def _dot_f32(a, b):
    return jnp.dot(a, b, preferred_element_type=F32, precision=lax.Precision.HIGHEST)


def _dot_bf16x3(a, b):
    ah, bh = a.astype(BF16), b.astype(BF16)
    al, bl = (a - ah.astype(F32)).astype(BF16), (b - bh.astype(F32)).astype(BF16)
    return _dot(ah, bh) + (_dot(ah, bl) + _dot(al, bh))


def _dot_nt_f32(a, b):
    return lax.dot_general(a, b, (((1,), (1,)), ((), ())), preferred_element_type=F32,
                           precision=lax.Precision.HIGHEST)


def _ada_kernel(c_ref, w_ref, b_ref, o_ref):
    c = c_ref[...]
    o_ref[...] = _dot_f32(c * _sigmoid(c), w_ref[...]) + b_ref[...]


def _ada_call(c_all, w_ada, b_ada):
    depth, d, n = w_ada.shape
    rows = c_all.shape[0]
    tn = ADA_TILE_N
    return pl.pallas_call(
        _ada_kernel,
        out_shape=jax.ShapeDtypeStruct((depth, rows, n), F32),
        grid=(depth, n // tn),
        in_specs=[
            pl.BlockSpec((rows, d), lambda l, j: (0, 0)),
            pl.BlockSpec((None, d, tn), lambda l, j: (l, 0, j)),
            pl.BlockSpec((None, 1, tn), lambda l, j: (l, 0, j)),
        ],
        out_specs=pl.BlockSpec((None, rows, tn), lambda l, j: (l, 0, j)),
        compiler_params=_cparams(("arbitrary", "arbitrary")),
        name="adaln",
    )(c_all, w_ada, b_ada.reshape(depth, 1, n))


def _mod_spec(mod, tm, sub, d):
    if mod.shape[0] == 1:
        return pl.BlockSpec((1, 3 * d), lambda i: (0, sub))
    return pl.BlockSpec((tm, 3 * d), lambda i: (i, sub))


def _ffn_body(x, mod_ref, g_ref, wup_ref, wdn_ref, fc):
    d = x.shape[-1]
    f = wdn_ref.shape[0]
    h = _rms_mod(x, g_ref[...], mod_ref[:, 0:d], mod_ref[:, d:2 * d]).astype(BF16)
    acc = jnp.zeros(x.shape, F32)
    for c in range(f // fc):
        a = _dot(h, wup_ref[:, c * fc:(c + 1) * fc])
        g = _dot(h, wup_ref[:, f + c * fc:f + (c + 1) * fc])
        act = (g * _sigmoid(g) * a).astype(BF16)
        acc = acc + _dot(act, wdn_ref[c * fc:(c + 1) * fc, :])
    return x + 0.5 * mod_ref[:, 2 * d:3 * d] * acc


def _ffn_kernel(x_ref, mod_ref, g_ref, wup_ref, wdn_ref, o_ref, *, fc):
    o_ref[...] = _ffn_body(x_ref[...], mod_ref, g_ref, wup_ref, wdn_ref, fc)


def _outproj_ffn_kernel(x_ref, ya_ref, sgb_ref, yatt_ref, wout_ref, modo_ref, mod_ref, g_ref, wup_ref, wdn_ref,
                        o_ref, *, fc):
    x = x_ref[...]
    d = x.shape[-1]
    merged = (ya_ref[...] + sgb_ref[...] * yatt_ref[...]).astype(BF16)
    x = x + modo_ref[:, 2 * d:3 * d] * _dot(merged, wout_ref[...])
    o_ref[...] = _ffn_body(x, mod_ref, g_ref, wup_ref, wdn_ref, fc)


def _ffn_chunk(f):
    for fc in (512, 256, 128):
        if f % fc == 0:
            return fc
    return f


def _ffn_call(x, mod, sub, g, wup, wdn, tm):
    t, d = x.shape
    row = pl.BlockSpec((tm, d), lambda i: (i, 0))
    return pl.pallas_call(
        functools.partial(_ffn_kernel, fc=_ffn_chunk(wdn[0].shape[-2])),
        out_shape=jax.ShapeDtypeStruct((t, d), F32),
        grid=(t // tm,),
        in_specs=[row, _mod_spec(mod, tm, sub, d), _resident((1, d)), _stacked_weight_spec(wup),
                  _stacked_weight_spec(wdn)],
        out_specs=row,
        compiler_params=_cparams(("parallel",)),
        name="ffn",
    )(x, mod, g.reshape(1, d), wup[0], wdn[0])


def _outproj_ffn_call(x, ya, sgb, yatt, wout, mod, sub_out, sub, g, wup, wdn, tm):
    t, d = x.shape
    row = pl.BlockSpec((tm, d), lambda i: (i, 0))
    return pl.pallas_call(
        functools.partial(_outproj_ffn_kernel, fc=_ffn_chunk(wdn[0].shape[-2])),
        out_shape=jax.ShapeDtypeStruct((t, d), F32),
        grid=(t // tm,),
        in_specs=[row, row, row, row, _stacked_weight_spec(wout), _mod_spec(mod, tm, sub_out, d),
                  _mod_spec(mod, tm, sub, d), _resident((1, d)), _stacked_weight_spec(wup),
                  _stacked_weight_spec(wdn)],
        out_specs=row,
        compiler_params=_cparams(("parallel",)),
        name="outproj_ffn",
    )(x, ya, sgb, yatt, wout[0], mod, mod, g.reshape(1, d), wup[0], wdn[0])


def _head_rms(u, g, nheads, hd, scale):
    outs = []
    for h in range(nheads):
        uh = u[:, h * hd:(h + 1) * hd]
        ms = jnp.mean(uh * uh, axis=-1, keepdims=True)
        outs.append(uh * lax.rsqrt(ms + EPS) * g * scale)
    return outs


def _inproj_kernel(x_ref, mod_ref, g_ref, win_ref, qg_ref, kg_ref, *out_refs, dr, nh, nkv, hd, prompt, nprev):
    out_refs = out_refs[nprev:]
    if prompt:
        xr_ref, gg_ref, sgb_ref, q_ref, k_ref, v_ref, kb_ref, vt_ref, km_ref = out_refs
    else:
        xr_ref, gg_ref, sgb_ref, q_ref, k_ref, v_ref = out_refs
    x = x_ref[...]
    tm, d = x.shape
    h = _rms_mod(x, g_ref[...], mod_ref[:, 0:d], mod_ref[:, d:2 * d]).astype(BF16)
    dq, dkv = nh * hd, nkv * hd
    o_q = 2 * dr
    o_k = o_q + dq
    o_v = o_k + dkv
    o_ga = o_v + dkv
    o_gb = o_ga + d

    def seg(lo, width):
        return _dot(h, win_ref[:, lo:lo + width])

    xr_ref[...] = seg(0, dr)
    gg_ref[...] = _sigmoid(seg(o_ga, d)) * _gelu_tanh(seg(dr, dr))
    sgb_ref[...] = _sigmoid(seg(o_gb, d))
    qs = _head_rms(seg(o_q, dq), qg_ref[...], nh, hd, hd ** -0.5)
    for hh in range(nh):
        q_ref[:, hh * hd:(hh + 1) * hd] = qs[hh]
    ks = _head_rms(seg(o_k, dkv), kg_ref[...], nkv, hd, 1.0)
    v = seg(o_v, dkv)
    for hh in range(nkv):
        k_ref[pl.ds(hh, tm, stride=nkv), :] = ks[hh]
        v_ref[pl.ds(hh, tm, stride=nkv), :] = v[:, hh * hd:(hh + 1) * hd]
    if prompt:
        vt = v.T.astype(BF16)
        hv = hd + BF16_ROWS
        for hh in range(nkv):
            vt_ref[hh * hv:hh * hv + hd, :] = vt[hh * hd:(hh + 1) * hd, :]
            vt_ref[hh * hv + hd:(hh + 1) * hv, :] = jnp.ones((BF16_ROWS, tm), BF16)
            kb_ref[:, hh * hd:(hh + 1) * hd] = ks[hh].astype(BF16)
            for b in range(tm // MOBA_BLOCK):
                km_ref[b, :, hh * hd:(hh + 1) * hd] = jnp.mean(
                    ks[hh][b * MOBA_BLOCK:(b + 1) * MOBA_BLOCK], axis=0, keepdims=True)


def _inproj_call(x, mod, g, win, qg, kg, tm, *, dr, nh, nkv, hd, prompt, slot=0, nslots=1, kv_prev=()):
    t, d = x.shape
    dq, dkv = nh * hd, nkv * hd
    nt = t // tm
    row = lambda w: pl.BlockSpec((tm, w), lambda i: (i, 0))
    out_shape = [jax.ShapeDtypeStruct((t, w), F32) for w in (dr, d, d, dq)]
    out_specs = [row(w) for w in (dr, d, d, dq)]
    out_shape += [jax.ShapeDtypeStruct((nslots * t * nkv, hd), F32)] * 2
    out_specs += [pl.BlockSpec((tm * nkv, hd), lambda i: (slot * nt + i, 0))] * 2
    n_in = 6
    if prompt:
        nbt = tm // MOBA_BLOCK
        dvt = nkv * (hd + BF16_ROWS)
        out_shape += [jax.ShapeDtypeStruct((t, dkv), BF16), jax.ShapeDtypeStruct((dvt, t), BF16),
                      jax.ShapeDtypeStruct((t // MOBA_BLOCK, 1, dkv), F32)]
        out_specs += [row(dkv), pl.BlockSpec((dvt, tm), lambda i: (0, i)),
                      pl.BlockSpec((nbt, 1, dkv), lambda i: (i, 0, 0))]
    return pl.pallas_call(
        functools.partial(_inproj_kernel, dr=dr, nh=nh, nkv=nkv, hd=hd, prompt=prompt, nprev=len(kv_prev)),
        out_shape=out_shape,
        grid=(nt,),
        in_specs=[row(d), _mod_spec(mod, tm, 1, d), _resident((1, d)), _stacked_weight_spec(win),
                  _resident((1, hd)), _resident((1, hd))] + [pl.BlockSpec(memory_space=pl.ANY)] * len(kv_prev),
        out_specs=out_specs,
        input_output_aliases={n_in + j: 4 + j for j in range(len(kv_prev))},
        compiler_params=_cparams(("parallel",)),
        name="inproj",
    )(x, mod, g.reshape(1, d), win[0], qg.reshape(1, hd), kg.reshape(1, hd), *kv_prev)


def _rglru_gates(xc, wg_ref, ba_ref, bx_ref, lam_ref, a_scr, b_scr):
    ngroups, gw, _ = wg_ref.shape
    xcb = xc.astype(BF16)
    for gi in range(ngroups):
        cols = slice(gi * gw, (gi + 1) * gw)
        u = _dot(xcb[:, cols], wg_ref[gi])
        r = _sigmoid(u[:, :gw] + ba_ref[:, cols])
        gate_x = _sigmoid(u[:, gw:] + bx_ref[:, cols])
        lam = lam_ref[:, cols]
        softplus_neg = jnp.maximum(-lam, 0.0) + jnp.log(1.0 + jnp.exp(-jnp.abs(lam)))
        log_a = -LRU_C * r * softplus_neg
        a = jnp.exp(log_a)
        mult = jnp.sqrt(-_expm1(2.0 * log_a))
        a_scr[:, cols] = a
        b_scr[:, cols] = mult * gate_x * xc[:, cols]


def _expm1(x):
    small = x * (1.0 + x * (0.5 + x * (1.0 / 6.0 + x * (1.0 / 24.0 + x * (1.0 / 120.0)))))
    return jnp.where(x > -0.1, small, jnp.exp(x) - 1.0)


def _scan8(a, b):
    row = lax.broadcasted_iota(jnp.int32, a.shape, 0)
    for k in (1, 2, 4):
        keep = row >= k
        a_sh = jnp.where(keep, pltpu.roll(a, k, 0), 1.0)
        b_sh = jnp.where(keep, pltpu.roll(b, k, 0), 0.0)
        b = a * b_sh + b
        a = a * a_sh
    return a, b


def _rglru_prompt_kernel(xr_ref, gg_ref, cw_ref, cb_ref, wg_ref, ba_ref, bx_ref, lam_ref, ya_ref, hlast_ref,
                         xbuf, a_scr, b_scr, hcar):
    i = pl.program_id(0)
    tm, dr = xr_ref.shape
    cwid = cw_ref.shape[0]

    @pl.when(i == 0)
    def _():
        xbuf[0:SUBLANES, :] = jnp.zeros((SUBLANES, dr), F32)
        hcar[...] = jnp.zeros_like(hcar)

    xbuf[SUBLANES:SUBLANES + tm, :] = xr_ref[...]
    xb = xbuf[...]
    xc = cb_ref[...] + cw_ref[cwid - 1:cwid, :] * xb[SUBLANES:, :]
    for j in range(cwid - 1):
        xc = xc + cw_ref[j:j + 1, :] * pltpu.roll(xb, cwid - 1 - j, 0)[SUBLANES:, :]
    xbuf[0:SUBLANES, :] = xbuf[tm:tm + SUBLANES, :]
    _rglru_gates(xc, wg_ref, ba_ref, bx_ref, lam_ref, a_scr, b_scr)

    def body(g, carry):
        rows = pl.ds(pl.multiple_of(g * SUBLANES, SUBLANES), SUBLANES)
        a, b = _scan8(a_scr[rows, :], b_scr[rows, :])
        h = a * carry + b
        ya_ref[rows, :] = h * gg_ref[rows, :]
        return h[SUBLANES - 1:SUBLANES, :]

    hc = lax.fori_loop(0, tm // SUBLANES, body, hcar[...], unroll=4)
    hcar[...] = hc
    hlast_ref[...] = hc


def _rglru_prompt_call(xr, gg, cw, cb, wg, ba, bx, lam, tm):
    t, dr = xr.shape
    row = pl.BlockSpec((tm, dr), lambda i: (i, 0))
    vec = _resident((1, dr))
    return pl.pallas_call(
        _rglru_prompt_kernel,
        out_shape=[jax.ShapeDtypeStruct((t, dr), F32), jax.ShapeDtypeStruct((1, dr), F32)],
        grid=(t // tm,),
        in_specs=[row, row, _resident(cw.shape), vec, _resident(wg.shape), vec, vec, vec],
        out_specs=[row, pl.BlockSpec((1, dr), lambda i: (0, 0))],
        scratch_shapes=[pltpu.VMEM((tm + SUBLANES, dr), F32), pltpu.VMEM((tm, dr), F32),
                        pltpu.VMEM((tm, dr), F32), pltpu.VMEM((1, dr), F32)],
        compiler_params=_cparams(("arbitrary",)),
        name="rglru_prompt",
    )(xr, gg, cw, cb.reshape(1, dr), wg, ba.reshape(1, dr), bx.reshape(1, dr), lam.reshape(1, dr))


def _rglru_sample_kernel(xcat_ref, gg_ref, h0_ref, cw_ref, cb_ref, wg_ref, ba_ref, bx_ref, lam_ref, ya_ref, h_ref,
                         xc_scr, a_scr, b_scr):
    nseq, _, dr = xcat_ref.shape
    cwid = cw_ref.shape[0]

    def conv(s, _):
        xc = cb_ref[...] + jnp.zeros((SUBLANES, dr), F32)
        for j in range(cwid):
            xc = xc + cw_ref[j:j + 1, :] * xcat_ref[s, j:j + SUBLANES, :]
        xc_scr[pl.ds(pl.multiple_of(s * SUBLANES, SUBLANES), SUBLANES), :] = xc
        return 0

    lax.fori_loop(0, nseq, conv, 0)
    _rglru_gates(xc_scr[...], wg_ref, ba_ref, bx_ref, lam_ref, a_scr, b_scr)

    def body(s, _):
        rows = pl.ds(pl.multiple_of(s * SUBLANES, SUBLANES), SUBLANES)
        a, b = _scan8(a_scr[rows, :], b_scr[rows, :])
        h = a * h0_ref[s] + b
        h_ref[rows, :] = h
        ya_ref[rows, :] = h * gg_ref[rows, :]
        return 0

    lax.fori_loop(0, nseq, body, 0)


def _rglru_sample_call(xcat, gg, h0, cw, cb, wg, ba, bx, lam):
    nseq, _, dr = xcat.shape
    t = nseq * SUBLANES
    full = lambda shape: pl.BlockSpec(shape, lambda i: (0,) * len(shape))
    return pl.pallas_call(
        _rglru_sample_kernel,
        out_shape=[jax.ShapeDtypeStruct((t, dr), F32), jax.ShapeDtypeStruct((t, dr), F32)],
        grid=(1,),
        in_specs=[full(xcat.shape), full((t, dr)), full((nseq, 1, dr)), full(cw.shape), full((1, dr)),
                  full(wg.shape), full((1, dr)), full((1, dr)), full((1, dr))],
        out_specs=[full((t, dr)), full((t, dr))],
        scratch_shapes=[pltpu.VMEM((t, dr), F32), pltpu.VMEM((t, dr), F32), pltpu.VMEM((t, dr), F32)],
        compiler_params=_cparams(("arbitrary",)),
        name="rglru_sample",
    )(xcat, gg, h0.reshape(nseq, 1, dr), cw, cb.reshape(1, dr), wg, ba.reshape(1, dr), bx.reshape(1, dr),
      lam.reshape(1, dr))


def _select_blocks(gs, n_eligible, axis):
    nb = gs.shape[axis]
    jj = lax.broadcasted_iota(jnp.int32, gs.shape, axis)
    s = jnp.where(jj < n_eligible, gs, -jnp.inf)
    sel = jnp.zeros(gs.shape, F32)
    for p in range(MOBA_TOP_K):
        m = jnp.max(s, axis=axis, keepdims=True)
        idx = jnp.min(jnp.where(s == m, jj, nb), axis=axis, keepdims=True)
        hit = jj == idx
        sel = jnp.where(jnp.logical_and(hit, jj * 0 + p < n_eligible), 1.0, sel)
        s = jnp.where(hit, -jnp.inf, s)
    return sel


def _attn_prompt_kernel(q_ref, qn_ref, k_ref, vt_ref, km_ref, near_ref, e_ref, place_ref, o_ref,
                        qa_scr, fm_scr, fmn_scr, adjn_scr, m_scr, acc_scr, s0_scr, s1_scr):
    i = pl.program_id(1)
    tq, hd2 = q_ref.shape
    hd = hd2 // 2
    blk = MOBA_BLOCK
    cb = FAR_BLOCKS * blk
    nb = km_ref.shape[0]
    nchunk = nb // FAR_BLOCKS
    nslab = (nchunk + 1) * BF16_ROWS
    prow = lax.broadcasted_iota(jnp.int32, (nslab, 2 * tq), 0)
    chunk_masked = jnp.where(prow % BF16_ROWS < FAR_BLOCKS, NEG, 0.0)

    @pl.when(i == 0)
    def _():
        fmn_scr[...] = chunk_masked.astype(BF16)
        adjn_scr[...] = jnp.zeros((1, 2 * tq), F32)

    fm_scr[...] = fmn_scr[...]
    adj_sel = adjn_scr[...]

    def transposed(qr):
        q2 = qr[...]
        return jnp.concatenate([q2[:, :hd].T, q2[:, hd:].T], axis=1)

    qa_scr[0:hd, :] = (transposed(q_ref) * LOG2E).astype(BF16)
    qa_scr[hd + BF16_ROWS:2 * hd, :] = jnp.zeros((hd - BF16_ROWS, 2 * tq), BF16)
    qtb = qa_scr[0:hd, :]

    gate = _dot_bf16x3(km_ref[...], transposed(qn_ref))

    jn = pl.multiple_of(jnp.maximum(i - 1, 0) * blk, blk)
    first = i == 0
    tile_lo = jnp.where(first, near_ref[1], near_ref[0] + jnp.where(adj_sel > 0.5, 0.0, NEG))
    tile_hi = jnp.where(first, NEG, near_ref[1])
    s_near = _dot(k_ref[pl.ds(jn, 2 * blk), :], qtb) + jnp.concatenate([tile_lo, tile_hi], axis=0)

    nch = jnp.maximum(i + FAR_BLOCKS - 2, 0) // FAR_BLOCKS

    def chunk_offset(c):
        return pl.multiple_of(jnp.minimum(c, nchunk - 1) * cb, cb)

    def scores(c, s_ref):
        slab = pl.multiple_of(jnp.where(c < nch, c, nchunk) * BF16_ROWS, BF16_ROWS)
        qa_scr[hd:hd + BF16_ROWS, :] = fm_scr[pl.ds(slab, BF16_ROWS), :]
        kc = jnp.concatenate([k_ref[pl.ds(chunk_offset(c), cb), :], e_ref[...]], axis=1)
        st = _dot(kc, qa_scr[...])
        s_ref[0:cb, :] = st
        s_ref[cb:cb + 1, :] = jnp.max(st, axis=0, keepdims=True)

    def accumulate(c, s_ref):
        m_old = m_scr[...]
        m_new = jnp.maximum(m_old, s_ref[cb:cb + 1, :])
        p = jnp.exp2(s_ref[0:cb, :] - m_new).astype(BF16)
        acc_scr[...] = jnp.exp2(m_old - m_new) * acc_scr[...] + _dot(vt_ref[:, pl.ds(chunk_offset(c), cb)], p)
        m_scr[...] = m_new

    scores(0, s0_scr)
    m = jnp.max(s_near, axis=0, keepdims=True)
    p_near = jnp.exp2(s_near - m).astype(BF16)
    sel = _select_blocks(gate, i + 1, 0)
    acc_scr[...] = _dot(vt_ref[:, pl.ds(jn, 2 * blk)], p_near)
    m_scr[...] = m
    jj = lax.broadcasted_iota(jnp.int32, sel.shape, 0)
    adjn_scr[...] = jnp.max(jnp.where(jj == i, sel, 0.0), axis=0, keepdims=True)
    placed = _dot(place_ref[...], jnp.where(jj < i, sel, 0.0).astype(BF16))
    placed = jnp.concatenate([placed, jnp.zeros((BF16_ROWS, 2 * tq), F32)], axis=0)
    fmn_scr[...] = jnp.where(placed > 0.5, 0.0, chunk_masked).astype(BF16)

    def far_body(t, _):
        scores(2 * t + 1, s1_scr)
        accumulate(2 * t, s0_scr)
        scores(2 * t + 2, s0_scr)
        accumulate(2 * t + 1, s1_scr)
        return 0

    lax.fori_loop(0, jnp.maximum(nch - 1, 0) // 2, far_body, 0)

    @pl.when(nch % 2 == 1)
    def _():
        accumulate(nch - 1, s0_scr)

    @pl.when(jnp.logical_and(nch % 2 == 0, nch > 0))
    def _():
        scores(nch - 1, s1_scr)
        accumulate(nch - 2, s0_scr)
        accumulate(nch - 1, s1_scr)

    acc = acc_scr[...]
    ot = acc[:hd] / acc[hd:hd + 1]
    o_ref[:, :hd] = ot[:, :tq].T
    o_ref[:, hd:] = ot[:, tq:].T


def _attn_prompt_call(q, kb, vt, km, near, nkv, hd):
    s = q.shape[0]
    tq = MOBA_BLOCK
    nb = s // tq
    assert nb % FAR_BLOCKS == 0 and nb <= hd
    cb = FAR_BLOCKS * tq
    nchunk = nb // FAR_BLOCKS
    hv = hd + BF16_ROWS
    e = (jnp.arange(hd)[None, :] == (jnp.arange(cb) // tq)[:, None]).astype(BF16)
    pr = jnp.arange(nchunk * BF16_ROWS)
    place = ((pr % BF16_ROWS < FAR_BLOCKS)[:, None]
             & ((pr // BF16_ROWS * FAR_BLOCKS + pr % BF16_ROWS)[:, None] == jnp.arange(nb)[None, :])).astype(BF16)
    return pl.pallas_call(
        _attn_prompt_kernel,
        out_shape=jax.ShapeDtypeStruct(q.shape, F32),
        grid=(nkv, nb),
        in_specs=[
            pl.BlockSpec((tq, 2 * hd), lambda h, i: (i, h)),
            pl.BlockSpec((tq, 2 * hd), lambda h, i: (jnp.minimum(i + 1, nb - 1), h)),
            pl.BlockSpec((s, hd), lambda h, i: (0, h)),
            pl.BlockSpec((hv, s), lambda h, i: (h, 0)),
            pl.BlockSpec((nb, hd), lambda h, i: (0, h)),
            pl.BlockSpec((None, 2, tq, 2 * tq), lambda h, i: (h, 0, 0, 0)),
            _resident(e.shape),
            _resident(place.shape),
        ],
        out_specs=pl.BlockSpec((tq, 2 * hd), lambda h, i: (i, h)),
        scratch_shapes=[pltpu.VMEM((2 * hd, 2 * tq), BF16), pltpu.VMEM(((nchunk + 1) * BF16_ROWS, 2 * tq), BF16),
                        pltpu.VMEM(((nchunk + 1) * BF16_ROWS, 2 * tq), BF16),
                        pltpu.VMEM((1, 2 * tq), F32), pltpu.VMEM((1, 2 * tq), F32), pltpu.VMEM((hv, 2 * tq), F32),
                        pltpu.VMEM((cb + SUBLANES, 2 * tq), F32), pltpu.VMEM((cb + SUBLANES, 2 * tq), F32)],
        compiler_params=_cparams(("arbitrary", "arbitrary")),
        name="moba_prompt",
    )(q, q, kb, vt, km, near, e, place)


def _paged(layer, p, npg):
    def index(b, g, pt):
        return (layer, pt[b, g * npg + p], 0, 0)
    return index


def _pv_ones(p, v, ones):
    return _dot(p.astype(BF16), jnp.concatenate([v.astype(BF16), ones[:v.shape[0]]], axis=1))


def _sample_blocks_kernel(pt_ref, q_ref, base_ref, last_ref, *refs, nkv, ppb):
    npg = (len(refs) - 2) // 2
    kpages, vpages = refs[:npg], refs[npg:2 * npg]
    part_ref, km_ref = refs[2 * npg:]
    is_last = pl.program_id(1) == pl.num_programs(1) - 1
    rows, hd = q_ref.shape
    prow = kpages[0].shape[0]
    qb = (q_ref[...] * LOG2E).astype(BF16)
    ones = jnp.ones((prow, hd), BF16)
    lane = lax.broadcasted_iota(jnp.int32, (rows, 2 * hd), 1)
    bps = npg // ppb
    for b in range(bps):
        tot = jnp.zeros((SUBLANES, hd), F32)
        ss = []
        for r in range(ppb):
            kp = kpages[b * ppb + r][...]
            tot = tot + jnp.sum(kp.reshape(prow // SUBLANES, SUBLANES, hd), axis=0)
            tile = base_ref[...]
            if b == bps - 1:
                tile = jnp.where(is_last, last_ref[r], tile)
            ss.append(_dot_nt(qb, kp.astype(BF16)) + tile)
        tot = tot + pltpu.roll(tot, nkv, 0)
        km_ref[b] = tot * (1.0 / (ppb * (prow // nkv)))
        mx = ss[0]
        for s in ss[1:]:
            mx = jnp.maximum(mx, s)
        m = jnp.max(mx, axis=1, keepdims=True)
        part = _pv_ones(jnp.exp2(ss[0] - m), vpages[b * ppb][...], ones)
        for r in range(1, ppb):
            part = part + _pv_ones(jnp.exp2(ss[r] - m), vpages[b * ppb + r][...], ones)
        part_ref[b] = jnp.where(lane == 2 * hd - 1, m, part)


def _sample_blocks_call(qall, base, last, ck, cv, page_table, layer, nb, nkv):
    _, _, prow, hd = ck.shape
    bs, npages = page_table.shape
    ppb = npages // nb
    npg = PAGES_PER_STEP
    assert 2 * nkv == SUBLANES and npg % ppb == 0 and npages % npg == 0
    bps = npg // ppb
    rows = qall.shape[1]
    page_spec = lambda p: pl.BlockSpec((None, None, prow, hd), _paged(layer, p, npg))
    const = lambda shape: pl.BlockSpec(shape, lambda b, g, pt: (0,) * len(shape))
    grid_spec = pltpu.PrefetchScalarGridSpec(
        num_scalar_prefetch=1,
        grid=(bs, npages // npg),
        in_specs=[pl.BlockSpec((None, rows, hd), lambda b, g, pt: (b, 0, 0)), const(base.shape), const(last.shape)]
        + [page_spec(p) for p in range(npg)] * 2,
        out_specs=[pl.BlockSpec((None, bps, rows, 2 * hd), lambda b, g, pt: (b, g, 0, 0)),
                   pl.BlockSpec((None, bps, SUBLANES, hd), lambda b, g, pt: (b, g, 0, 0))],
    )
    return pl.pallas_call(
        functools.partial(_sample_blocks_kernel, nkv=nkv, ppb=ppb),
        out_shape=[jax.ShapeDtypeStruct((bs, nb, rows, 2 * hd), F32),
                   jax.ShapeDtypeStruct((bs, nb, SUBLANES, hd), F32)],
        grid_spec=grid_spec,
        compiler_params=_cparams(("parallel", "parallel")),
        name="sample_blocks",
    )(page_table, qall, base, last, *([ck] * npg), *([cv] * npg))


def _sample_merge_kernel(q_ref, kn_ref, vn_ref, km_ref, own_ref, part_ref, o_ref, *, nkv):
    rows, hd = q_ref.shape
    rph = rows // nkv
    nb = km_ref.shape[1]
    q = q_ref[...]
    qb = (q * LOG2E).astype(BF16)
    sel = jnp.concatenate(
        [_select_blocks(_dot_nt_f32(q[h * rph:(h + 1) * rph], km_ref[h]), nb, 1) for h in range(nkv)], axis=0)
    pad = jnp.zeros((LANES - kn_ref.shape[0], hd), F32)
    s = _dot_nt(qb, jnp.concatenate([kn_ref[...], pad], axis=0).astype(BF16)) + own_ref[...]
    m0 = jnp.max(s, axis=1, keepdims=True)
    acc0 = _pv_ones(jnp.exp2(s - m0), jnp.concatenate([vn_ref[...], pad], axis=0), jnp.ones((LANES, hd), BF16))
    lane = lax.broadcasted_iota(jnp.int32, (rows, nb), 1)
    mb = jnp.zeros((rows, nb), F32)
    for j in range(nb):
        mb = jnp.where(lane == j, part_ref[j, :, 2 * hd - 1:2 * hd], mb)
    picked = sel > 0.5
    m = jnp.maximum(m0, jnp.max(jnp.where(picked, mb, NEG), axis=1, keepdims=True))
    w = jnp.where(picked, jnp.exp2(mb - m), 0.0)
    acc = jnp.exp2(m0 - m) * acc0
    for j in range(nb):
        acc = acc + w[:, j:j + 1] * part_ref[j]
    o_ref[...] = acc[:, :hd] / acc[:, hd:hd + 1]


def _sample_merge_call(qall, kn, vn, km, own, parts, nkv):
    bs, rows, hd = qall.shape
    nb = km.shape[2]
    nown = kn.shape[1]
    assert nown <= LANES
    per_seq = lambda *shape: pl.BlockSpec((None,) + shape, lambda b: (b,) + (0,) * len(shape))
    return pl.pallas_call(
        functools.partial(_sample_merge_kernel, nkv=nkv),
        out_shape=jax.ShapeDtypeStruct(qall.shape, F32),
        grid=(bs,),
        in_specs=[per_seq(rows, hd), per_seq(nown, hd), per_seq(nown, hd), per_seq(nkv, nb, hd),
                  pl.BlockSpec(own.shape, lambda b: (0, 0)), per_seq(nb, rows, 2 * hd)],
        out_specs=per_seq(rows, hd),
        compiler_params=_cparams(("parallel",)),
        name="sample_merge",
    )(qall, kn, vn, km, own, parts)


def _rel_bucket(dist, n_buckets):
    n = jnp.maximum(dist, 0)
    max_exact = n_buckets // 2
    nf = jnp.maximum(n, 1).astype(F32)
    large = max_exact + (jnp.log(nf / max_exact) / math.log(MAX_DISTANCE / max_exact)
                         * (n_buckets - max_exact)).astype(jnp.int32)
    large = jnp.minimum(large, n_buckets - 1)
    return jnp.where(n < max_exact, n, large)


def _shifted_bias(dist, rel_table):
    nbk = rel_table.shape[0]
    far = rel_table[_rel_bucket(jnp.asarray(2 * MOBA_BLOCK), nbk)]
    b = (rel_table[_rel_bucket(dist, nbk)] - far) * LOG2E
    return jnp.where((dist >= 0)[..., None], b, NEG)


def _skew_toeplitz(u, n):
    lead = u.shape[:-1]
    up = jnp.concatenate([u, jnp.zeros(lead + (1,), u.dtype)], axis=-1)
    flat = jnp.broadcast_to(up[..., None, :], lead + (n, 2 * n)).reshape(lead + (2 * n * n,))
    return flat[..., :n * (2 * n - 1)].reshape(lead + (n, 2 * n - 1))


def _prompt_near_tiles(rel_table, nkv):
    blk = MOBA_BLOCK
    x = jnp.arange(2 * blk - 1) - (blk - 1)
    u = jnp.stack([_shifted_bias(x + blk, rel_table).T, _shifted_bias(x, rel_table).T])
    a = _skew_toeplitz(u, blk)[..., blk - 1:2 * blk - 1]
    a = a.reshape(2, nkv, 2, blk, blk)
    return jnp.transpose(a, (1, 0, 3, 2, 4)).reshape(nkv, 2, blk, 2 * blk)


def _sample_tiles(rel_table, nkv, lq, psz, ppb):
    rows = 2 * nkv * lq
    row_kvh = jnp.arange(rows) // (2 * lq)
    row_t = jnp.arange(rows) % lq
    col_h = jnp.arange(psz * nkv) % nkv
    base = jnp.where(row_kvh[:, None] == col_h[None, :], 0.0, NEG).astype(F32)
    kk = jnp.arange(ppb * psz)
    b = _shifted_bias(jnp.arange(lq)[:, None] + ppb * psz - kk[None, :], rel_table)
    b = jnp.transpose(b, (2, 0, 1)).reshape(rows, ppb, psz)
    last = jnp.transpose(b, (1, 0, 2))[:, :, :, None] + base.reshape(1, rows, psz, nkv)
    last = last.reshape(ppb, rows, psz * nkv)
    ncol = LANES
    col_t, col_hh = jnp.arange(ncol) // nkv, jnp.arange(ncol) % nkv
    ob = _shifted_bias(jnp.arange(lq)[:, None] - col_t[None, :], rel_table)
    ob = jnp.transpose(ob, (2, 0, 1)).reshape(rows, ncol)
    ok = (row_kvh[:, None] == col_hh[None, :]) & (col_t[None, :] < lq)
    own = jnp.where(ok, ob, NEG)
    return base, last, own


def _block_diag_gates(w_a, w_x, group):
    n, c, _ = w_a.shape
    eye = jnp.eye(group, dtype=w_a.dtype)

    def bd(w):
        w = w.reshape(n // group, group, c, c)
        return jnp.einsum('gbcd,be->gbced', w, eye).reshape(n // group, group * c, group * c)

    return jnp.concatenate([bd(w_a), bd(w_x)], axis=-1).astype(BF16)


def kernel(x_prompt, x_sample, c_prompt, c_sample, cache_k, cache_v, page_table, state_conv, state_rglru, norm_g,
           w_ada, b_ada, w_in, conv_w, conv_b, w_rg_a, b_rg_a, w_rg_x, b_rg_x, lru_lambda, q_norm_g, k_norm_g,
           rel_table, w_out, w_ffn_in, w_ffn_out):
    depth = norm_g.shape[0]
    bp, s, d = x_prompt.shape
    bs, lq, _ = x_sample.shape
    _, npool, psz, nkv, hd = cache_k.shape
    nh = rel_table.shape[1]
    dr = conv_w.shape[2]
    cwid = conv_w.shape[1]
    past_len = page_table.shape[1] * psz
    nb_past = past_len // MOBA_BLOCK
    assert bp == 1 and s % MOBA_BLOCK == 0 and lq == SUBLANES and nh == 2 * nkv
    assert past_len == nb_past * MOBA_BLOCK and MOBA_BLOCK % psz == 0 and cwid - 1 <= lq
    assert MAX_DISTANCE <= MOBA_BLOCK
    dkv = nkv * hd
    ppb = MOBA_BLOCK // psz
    tm_p = TOKEN_TILE if s % TOKEN_TILE == 0 else MOBA_BLOCK
    ts = bs * lq

    c_all = jnp.concatenate([c_prompt, c_sample], axis=0)
    nrow = -(-c_all.shape[0] // SUBLANES) * SUBLANES
    c_all = jnp.pad(c_all, ((0, nrow - c_all.shape[0]), (0, 0)))
    mods = _ada_call(c_all, w_ada, b_ada)

    ck = cache_k.reshape(depth, npool, psz * nkv, hd)
    cv = cache_v.reshape(depth, npool, psz * nkv, hd)
    near_p = _prompt_near_tiles(rel_table, nkv)
    base_s, last_s, own_s = _sample_tiles(rel_table, nkv, lq, psz, ppb)

    wup_all, wdn_all = w_ffn_in.astype(BF16), w_ffn_out.astype(BF16)
    win_all, wout_all = w_in.astype(BF16), w_out.astype(BF16)

    xp = x_prompt.reshape(s, d)
    xs = x_sample.reshape(ts, d)
    outs = {k: [] for k in ('cp', 'hp', 'ks', 'vs', 'cs', 'hs')}
    kv_p = ()
    for l in range(depth):
        mod_p = mods[l, 0:bp]
        mod_s = jnp.repeat(mods[l, bp:bp + bs], lq, axis=0)
        wup0, wup1 = (wup_all, (l, 0)), (wup_all, (l, 1))
        wdn0, wdn1 = (wdn_all, (l, 0)), (wdn_all, (l, 1))
        win, wout = (win_all, (l,)), (wout_all, (l,))
        wg = _block_diag_gates(w_rg_a[l], w_rg_x[l], 4)
        rg = (conv_w[l], conv_b[l], wg, b_rg_a[l], b_rg_x[l], lru_lambda[l])
        proj = dict(dr=dr, nh=nh, nkv=nkv, hd=hd)

        xp = _ffn_call(xp, mod_p, 0, norm_g[l, 0], wup0, wdn0, tm_p)
        xr, gg, sgb, q, *kv_p, kb, vt, km = _inproj_call(xp, mod_p, norm_g[l, 1], win, q_norm_g[l], k_norm_g[l], tm_p,
                                                         prompt=True, slot=l, nslots=depth, kv_prev=kv_p, **proj)
        ya, hlast = _rglru_prompt_call(xr, gg, *rg, tm_p)
        yatt = _attn_prompt_call(q, kb, vt, km.reshape(s // MOBA_BLOCK, dkv), near_p, nkv, hd)
        xp = _outproj_ffn_call(xp, ya, sgb, yatt, wout, mod_p, 1, 2, norm_g[l, 2], wup1, wdn1, tm_p)
        outs['cp'].append(xr[s - (cwid - 1):].reshape(bp, cwid - 1, dr))
        outs['hp'].append(hlast)

        xs = _ffn_call(xs, mod_s, 0, norm_g[l, 0], wup0, wdn0, ts)
        xr, gg, sgb, q, k, v = _inproj_call(xs, mod_s, norm_g[l, 1], win, q_norm_g[l], k_norm_g[l], ts,
                                            prompt=False, **proj)
        xr3 = xr.reshape(bs, lq, dr)
        xcat = jnp.concatenate([state_conv[l], xr3, jnp.zeros((bs, 2 * SUBLANES - lq - (cwid - 1), dr), F32)], axis=1)
        ya, h_all = _rglru_sample_call(xcat, gg, state_rglru[l], *rg)
        qall = jnp.transpose(q.reshape(bs, lq, nkv, 2, hd), (0, 2, 3, 1, 4)).reshape(bs, 2 * nkv * lq, hd)
        parts, km_s = _sample_blocks_call(qall, base_s, last_s, ck, cv, page_table, l, nb_past, nkv)
        km_s = jnp.transpose(km_s[:, :, :nkv], (0, 2, 1, 3))
        yatt = _sample_merge_call(qall, k.reshape(bs, lq * nkv, hd), v.reshape(bs, lq * nkv, hd), km_s,
                                  own_s, parts, nkv)
        yatt = jnp.transpose(yatt.reshape(bs, nkv, 2, lq, hd), (0, 3, 1, 2, 4)).reshape(ts, nh * hd)
        xs = _outproj_ffn_call(xs, ya, sgb, yatt, wout, mod_s, 1, 2, norm_g[l, 2], wup1, wdn1, ts)
        outs['ks'].append(k.reshape(bs, lq, nkv, hd))
        outs['vs'].append(v.reshape(bs, lq, nkv, hd))
        outs['cs'].append(xr3[:, lq - (cwid - 1):])
        outs['hs'].append(h_all.reshape(bs, lq, dr)[:, lq - 1])

    st = {k: jnp.stack(v) for k, v in outs.items()}
    k_p, v_p = (a.reshape(depth, bp, s, nkv, hd) for a in kv_p)
    return (xp.reshape(bp, s, d), xs.reshape(bs, lq, d), k_p, v_p, st['cp'], st['hp'],
            st['ks'], st['vs'], st['cs'], st['hs'])
```
